```python
import math
import jax
import jax.numpy as jnp
from jax import lax
import numpy as np

D_MODEL = 4096
BATCH = 2
SEQ = 4096
DEPTH = 1

D_RWKV = D_MODEL // 2
RWKV_HEAD = 64
N_RWKV_HEADS = D_RWKV // RWKV_HEAD
RANK_DECAY = max(32, int(round(1.8 * D_RWKV ** 0.5 / 32)) * 32)
RANK_ICLR = max(32, int(round(1.8 * D_RWKV ** 0.5 / 32)) * 32)
RANK_GATE = max(32, int(round(0.6 * D_RWKV ** 0.8 / 32)) * 32)
N_RWKV_COLS = 3 * D_RWKV + RANK_DECAY + RANK_ICLR + RANK_GATE
EPS_GN = 64e-5

D_DIFF = D_MODEL // 2
DIFF_HEAD = 64
N_DIFF_HEADS = D_DIFF // (2 * DIFF_HEAD)
DIFF_VDIM = 2 * DIFF_HEAD
Q_BLOCK = 128
EPS_SUBLN = 1e-5

N_BRANCHES = 2
N_IN_COLS = N_RWKV_COLS + 3 * D_DIFF + N_BRANCHES * D_MODEL
D_FF = 4 * D_MODEL
EPS_RMS = 1e-6

kernel_name = "hybrid_rwkv7_diffattn_gated_encoder_block"


def _rmsnorm(t, g, eps=EPS_RMS):
    tf = t.astype(jnp.float32)
    y = tf * lax.rsqrt(jnp.mean(tf * tf, axis=-1, keepdims=True) + eps)
    return (y * g).astype(t.dtype)


def _wkv7_scan(r, w, k, v, kk, a, reverse):
    B, S, H, N = r.shape
    xs = tuple(jnp.moveaxis(t, 1, 0) for t in (r, w, k, v, kk, a))

    def step(state, inp):
        r_t, w_t, k_t, v_t, kk_t, a_t = inp
        sa = jnp.einsum('bhij,bhj->bhi', state, -kk_t)
        state = (state * w_t[:, :, None, :]
                 + sa[..., None] * (kk_t * a_t)[:, :, None, :]
                 + v_t[..., None] * k_t[:, :, None, :])
        o_t = jnp.einsum('bhij,bhj->bhi', state, r_t)
        return state, o_t

    init = jnp.zeros((B, H, N, N), jnp.float32)
    _, out = lax.scan(step, init, xs, reverse=reverse)
    return jnp.moveaxis(out, 0, 1)


def _rwkv7_bidir(p, mu_prev, mu_next, w0f, w2f, w0b, w2b, a0f, a2f, a0b, a2b,
                 g2, k_k, k_a, r_k, ln_g, ln_b):
    B, S, _ = p.shape
    p = p.astype(jnp.float32)
    prev = jnp.pad(p[:, :-1], ((0, 0), (1, 0), (0, 0)))
    nxt = jnp.pad(p[:, 1:], ((0, 0), (0, 1), (0, 0)))
    p = p + mu_prev * (prev - p) + mu_next * (nxt - p)
    splits = [D_RWKV, 2 * D_RWKV, 3 * D_RWKV, 3 * D_RWKV + RANK_DECAY,
              3 * D_RWKV + RANK_DECAY + RANK_ICLR]
    r, k, v, dw, da, dg = jnp.split(p, splits, axis=-1)

    def heads(t):
        return t.reshape(B, S, N_RWKV_HEADS, RWKV_HEAD)

    g = jax.nn.sigmoid(dg) @ g2
    kk = heads(k * k_k)
    kk = kk * lax.rsqrt(jnp.maximum(jnp.sum(kk * kk, axis=-1, keepdims=True), 1e-24))
    tw = jnp.tanh(dw)
    rh, vh = heads(r), heads(v)

    def direction(w0, w2, a0, a2, reverse):
        log_w = -jax.nn.softplus(-(w0 + tw @ w2)) - 0.5
        w = jnp.exp(-jnp.exp(log_w))
        a = jax.nn.sigmoid(a0 + da @ a2)
        kd = k * (1.0 + (a - 1.0) * k_a)
        o = _wkv7_scan(rh, heads(w), heads(kd), vh, kk, heads(a), reverse)
        return o, kd

    o_f, k_f = direction(w0f, w2f, a0f, a2f, False)
    o_b, k_b = direction(w0b, w2b, a0b, a2b, True)
    o = o_f + o_b
    mu = jnp.mean(o, axis=-1, keepdims=True)
    var = jnp.mean(jnp.square(o - mu), axis=-1, keepdims=True)
    on = ((o - mu) * lax.rsqrt(var + EPS_GN)).reshape(B, S, D_RWKV) * ln_g + ln_b
    bonus = jnp.sum(rh * heads(0.5 * (k_f + k_b)) * r_k, axis=-1, keepdims=True) * vh
    return (on + bonus.reshape(B, S, D_RWKV)) * g


def _diff_attention(q, k, v, lq1, lk1, lq2, lk2, subln_g, lambda_init):
    B, S, _ = q.shape
    H = N_DIFF_HEADS
    q = q.astype(jnp.float32).reshape(B, S, H, 2, DIFF_HEAD)
    k = k.astype(jnp.float32).reshape(B, S, H, 2, DIFF_HEAD)
    v = v.astype(jnp.float32).reshape(B, S, H, DIFF_VDIM)
    lam = (jnp.exp(jnp.sum(lq1 * lk1).astype(jnp.float32))
           - jnp.exp(jnp.sum(lq2 * lk2).astype(jnp.float32)) + lambda_init)
    scale = DIFF_HEAD ** -0.5
    slopes = jnp.exp2(-8.0 * jnp.arange(1, H + 1, dtype=jnp.float32) / H)
    kpos = jnp.arange(S, dtype=jnp.int32)
    n_blk = S // Q_BLOCK
    qb = jnp.moveaxis(q.reshape(B, n_blk, Q_BLOCK, H, 2, DIFF_HEAD), 1, 0)
    starts = jnp.arange(n_blk, dtype=jnp.int32) * Q_BLOCK

    def block(args):
        q_blk, start = args
        qpos = start + jnp.arange(Q_BLOCK, dtype=jnp.int32)
        dist = jnp.abs(qpos[:, None] - kpos[None, :]).astype(jnp.float32)
        bias = -slopes[:, None, None] * dist[None]
        s = jnp.einsum('bqhcd,bkhcd->bchqk', q_blk, k) * scale + bias[None, None]
        pr = jax.nn.softmax(s, axis=-1)
        attn = pr[:, 0] - lam * pr[:, 1]
        return jnp.einsum('bhqk,bkhe->bqhe', attn, v)

    out = lax.map(block, (qb, starts))
    out = jnp.moveaxis(out, 0, 1).reshape(B, S, H, DIFF_VDIM)
    out = out * lax.rsqrt(jnp.mean(out * out, axis=-1, keepdims=True) + EPS_SUBLN) * subln_g
    return (out * (1.0 - lambda_init)).reshape(B, S, D_DIFF)


def setup_inputs(seed: int = 0) -> dict:
    key = jax.random.key(seed)
    ks = iter(jax.random.split(key, 48))
    L = DEPTH

    def normal(shape, scale):
        return jax.random.normal(next(ks), shape, jnp.float32) * scale

    def gain(shape):
        return 1.0 + normal(shape, 0.02)

    def unif(shape, lo, hi):
        return jax.random.uniform(next(ks), shape, jnp.float32, minval=lo, maxval=hi)

    return {
        "x": normal((BATCH, SEQ, D_MODEL), 1.0),
        "attn_pre_norm": gain((L, D_MODEL)),
        "attn_post_norm": gain((L, D_MODEL)),
        "w_in": normal((L, D_MODEL, N_IN_COLS), D_MODEL ** -0.5),
        "shift_prev": unif((L, N_RWKV_COLS), 0.0, 0.5),
        "shift_next": unif((L, N_RWKV_COLS), 0.0, 0.5),
        "decay_bias_fwd": unif((L, D_RWKV), -6.0, -1.0),
        "decay_up_fwd": normal((L, RANK_DECAY, D_RWKV), 0.1 * RANK_DECAY ** -0.5),
        "decay_bias_bwd": unif((L, D_RWKV), -6.0, -1.0),
        "decay_up_bwd": normal((L, RANK_DECAY, D_RWKV), 0.1 * RANK_DECAY ** -0.5),
        "iclr_bias_fwd": normal((L, D_RWKV), 0.1),
        "iclr_up_fwd": normal((L, RANK_ICLR, D_RWKV), 0.1 * RANK_ICLR ** -0.5),
        "iclr_bias_bwd": normal((L, D_RWKV), 0.1),
        "iclr_up_bwd": normal((L, RANK_ICLR, D_RWKV), 0.1 * RANK_ICLR ** -0.5),
        "gate_up": normal((L, RANK_GATE, D_RWKV), RANK_GATE ** -0.5),
        "k_k": 0.85 + normal((L, D_RWKV), 0.02),
        "k_a": 1.0 + normal((L, D_RWKV), 0.02),
        "r_k": normal((L, N_RWKV_HEADS, RWKV_HEAD), 0.1),
        "ln_x_gain": gain((L, D_RWKV)),
        "ln_x_bias": normal((L, D_RWKV), 0.01),
        "lambda_q1": normal((L, DIFF_HEAD), 0.1),
        "lambda_k1": normal((L, DIFF_HEAD), 0.1),
        "lambda_q2": normal((L, DIFF_HEAD), 0.1),
        "lambda_k2": normal((L, DIFF_HEAD), 0.1),
        "subln_gain": gain((L, DIFF_VDIM)),
        "w_up_rwkv": normal((L, D_RWKV, D_MODEL), D_RWKV ** -0.5),
        "w_up_diff": normal((L, D_DIFF, D_MODEL), D_DIFF ** -0.5),
        "w_out": normal((L, D_MODEL, D_MODEL), D_MODEL ** -0.5),
        "mlp_pre_norm": gain((L, D_MODEL)),
        "mlp_post_norm": gain((L, D_MODEL)),
        "w_mlp_in": normal((L, D_MODEL, D_FF), D_MODEL ** -0.5),
        "w_mlp_out": normal((L, D_FF, D_MODEL), D_FF ** -0.5),
    }


def reference(x, attn_pre_norm, attn_post_norm, w_in, shift_prev, shift_next,
              decay_bias_fwd, decay_up_fwd, decay_bias_bwd, decay_up_bwd,
              iclr_bias_fwd, iclr_up_fwd, iclr_bias_bwd, iclr_up_bwd, gate_up,
              k_k, k_a, r_k, ln_x_gain, ln_x_bias,
              lambda_q1, lambda_k1, lambda_q2, lambda_k2, subln_gain,
              w_up_rwkv, w_up_diff, w_out, mlp_pre_norm, mlp_post_norm,
              w_mlp_in, w_mlp_out):
    col_splits = [N_RWKV_COLS, N_RWKV_COLS + D_DIFF, N_RWKV_COLS + 2 * D_DIFF,
                  N_RWKV_COLS + 3 * D_DIFF, N_RWKV_COLS + 3 * D_DIFF + D_MODEL]
    for l in range(DEPTH):
        lambda_init = 0.8 - 0.6 * math.exp(-0.3 * l)
        h = _rmsnorm(x, attn_pre_norm[l])
        proj = h @ w_in[l]
        p_rwkv, q, k, v, gate_a, gate_b = jnp.split(proj, col_splits, axis=-1)
        y_a = _rwkv7_bidir(p_rwkv, shift_prev[l], shift_next[l],
                           decay_bias_fwd[l], decay_up_fwd[l], decay_bias_bwd[l], decay_up_bwd[l],
                           iclr_bias_fwd[l], iclr_up_fwd[l], iclr_bias_bwd[l], iclr_up_bwd[l],
                           gate_up[l], k_k[l], k_a[l], r_k[l], ln_x_gain[l], ln_x_bias[l]).astype(x.dtype)
        y_b = _diff_attention(q, k, v, lambda_q1[l], lambda_k1[l], lambda_q2[l], lambda_k2[l],
                              subln_gain[l], lambda_init).astype(x.dtype)
        mixed = (jax.nn.sigmoid(gate_a) * (y_a @ w_up_rwkv[l])
                 + jax.nn.sigmoid(gate_b) * (y_b @ w_up_diff[l]))
        x = x + _rmsnorm(mixed @ w_out[l], attn_post_norm[l])
        h = _rmsnorm(x, mlp_pre_norm[l])
        u = jnp.square(jax.nn.relu(h @ w_mlp_in[l]))
        x = x + _rmsnorm(u @ w_mlp_out[l], mlp_post_norm[l])
    return x
```

```python
import functools
import math

import jax
import jax.numpy as jnp
from jax import lax
from jax.experimental import pallas as pl
from jax.experimental.pallas import tpu as pltpu

F32 = jnp.float32
BF16 = jnp.bfloat16

D_MODEL = 4096
D_RWKV = D_MODEL // 2
RWKV_HEAD = 64
RANK_LORA = 96
RANK_LORA_PAD = 128
RANK_GATE = 256
D_DIFF = D_MODEL // 2
DIFF_HEAD = 64
N_DIFF_HEADS = D_DIFF // (2 * DIFF_HEAD)
DIFF_VDIM = 2 * DIFF_HEAD
D_FF = 4 * D_MODEL
EPS_RMS = 1e-6
EPS_GN = 64e-5
EPS_SUBLN = 1e-5
LAMBDA_INIT = 0.8 - 0.6 * math.exp(-0.3 * 0)

HEAD_GROUP = 256
N_HEAD_GROUPS = D_RWKV // HEAD_GROUP
COL_RKV = 0
COL_LORA = 3 * D_RWKV
LORA_W = 2 * RANK_LORA_PAD + RANK_GATE
COL_QKV = COL_LORA + LORA_W
COL_GATE = COL_QKV + 3 * D_DIFF
N_PROJ = COL_GATE + 2 * D_MODEL

CHUNK = 64
VMEM_LIMIT = 48 * 1024 * 1024


def _cparams(sem):
    return pltpu.CompilerParams(dimension_semantics=sem, vmem_limit_bytes=VMEM_LIMIT)


def _rms(t, g):
    return t * lax.rsqrt(jnp.mean(t * t, axis=-1, keepdims=True) + EPS_RMS) * g


def _prenorm_body(x_ref, g_ref, o_ref):
    o_ref[...] = _rms(x_ref[...], g_ref[...]).astype(o_ref.dtype)


def _prenorm(x, g, tr=256):
    m, d = x.shape
    return pl.pallas_call(
        _prenorm_body,
        grid=(m // tr,),
        in_specs=[pl.BlockSpec((tr, d), lambda i: (i, 0)), pl.BlockSpec((1, d), lambda i: (0, 0))],
        out_specs=pl.BlockSpec((tr, d), lambda i: (i, 0)),
        out_shape=jax.ShapeDtypeStruct((m, d), BF16),
        compiler_params=_cparams(("parallel",)),
        name="prenorm",
    )(x, g.reshape(1, d))


def _resnorm_body(x_ref, z_ref, g_ref, g2_ref, o_ref, h_ref):
    y = x_ref[...] + _rms(z_ref[...], g_ref[...])
    o_ref[...] = y
    if h_ref is not None:
        h_ref[...] = _rms(y, g2_ref[...]).astype(h_ref.dtype)


def _resnorm(x, z, g, g_next=None, tr=256):
    m, d = x.shape
    row = pl.BlockSpec((tr, d), lambda i: (i, 0))
    vec = pl.BlockSpec((1, d), lambda i: (0, 0))
    if g_next is None:
        body = lambda x_ref, z_ref, g_ref, o_ref: _resnorm_body(x_ref, z_ref, g_ref, None, o_ref, None)
        return pl.pallas_call(
            body, grid=(m // tr,), in_specs=[row, row, vec], out_specs=row,
            out_shape=jax.ShapeDtypeStruct((m, d), F32),
            compiler_params=_cparams(("parallel",)), name="resnorm_out",
        )(x, z, g.reshape(1, d))
    return pl.pallas_call(
        _resnorm_body, grid=(m // tr,), in_specs=[row, row, vec, vec], out_specs=[row, row],
        out_shape=[jax.ShapeDtypeStruct((m, d), F32), jax.ShapeDtypeStruct((m, d), BF16)],
        compiler_params=_cparams(("parallel",)), name="resnorm_mid",
    )(x, z, g.reshape(1, d), g_next.reshape(1, d))


def _mm_body(a_ref, b_ref, o_ref, *, act):
    acc = jnp.dot(a_ref[...], b_ref[...], preferred_element_type=F32)
    if act == "relu2":
        acc = jnp.square(jnp.maximum(acc, 0.0))
    o_ref[...] = acc.astype(o_ref.dtype)


def _matmul(a, b, out_dtype, tm, tn, act=None, name="matmul"):
    m, k = a.shape
    _, n = b.shape
    return pl.pallas_call(
        functools.partial(_mm_body, act=act),
        grid=(m // tm, n // tn),
        in_specs=[pl.BlockSpec((tm, k), lambda i, j: (i, 0)), pl.BlockSpec((k, tn), lambda i, j: (0, j))],
        out_specs=pl.BlockSpec((tm, tn), lambda i, j: (i, j)),
        out_shape=jax.ShapeDtypeStruct((m, n), out_dtype),
        compiler_params=_cparams(("parallel", "arbitrary")),
        name=name,
    )(a, b)


def _mmk_body(a_ref, b_ref, o_ref, acc_ref):
    kk = pl.program_id(2)

    @pl.when(kk == 0)
    def _():
        acc_ref[...] = jnp.zeros_like(acc_ref)

    acc_ref[...] += jnp.dot(a_ref[...], b_ref[...], preferred_element_type=F32)

    @pl.when(kk == pl.num_programs(2) - 1)
    def _():
        o_ref[...] = acc_ref[...].astype(o_ref.dtype)


def _matmul_ktiled(a, b, out_dtype, tm, tn, tk, name="matmul_k"):
    m, k = a.shape
    _, n = b.shape
    return pl.pallas_call(
        _mmk_body,
        grid=(m // tm, n // tn, k // tk),
        in_specs=[pl.BlockSpec((tm, tk), lambda i, j, q: (i, q)), pl.BlockSpec((tk, tn), lambda i, j, q: (q, j))],
        out_specs=pl.BlockSpec((tm, tn), lambda i, j, q: (i, j)),
        out_shape=jax.ShapeDtypeStruct((m, n), out_dtype),
        scratch_shapes=[pltpu.VMEM((tm, tn), F32)],
        compiler_params=_cparams(("parallel", "parallel", "arbitrary")),
        name=name,
    )(a, b)


def _sigmoid(t):
    return 1.0 / (1.0 + jnp.exp(-t))


def _upgate_body(ya_ref, yb_ref, wa_ref, wb_ref, ga_ref, gb_ref, o_ref):
    acc_a = jnp.dot(ya_ref[...], wa_ref[...], preferred_element_type=F32)
    acc_b = jnp.dot(yb_ref[...], wb_ref[...], preferred_element_type=F32)
    ga = _sigmoid(ga_ref[...].astype(F32))
    gb = _sigmoid(gb_ref[...].astype(F32))
    o_ref[...] = (ga * acc_a + gb * acc_b).astype(o_ref.dtype)


def _upgate(ya, yb, wa, wb, proj, tm=1024, tn=512):
    m, k = ya.shape
    n = wa.shape[1]
    ga0 = COL_GATE // tn
    gb0 = (COL_GATE + D_MODEL) // tn
    return pl.pallas_call(
        _upgate_body,
        grid=(m // tm, n // tn),
        in_specs=[
            pl.BlockSpec((tm, k), lambda i, j: (i, 0)),
            pl.BlockSpec((tm, k), lambda i, j: (i, 0)),
            pl.BlockSpec((k, tn), lambda i, j: (0, j)),
            pl.BlockSpec((k, tn), lambda i, j: (0, j)),
            pl.BlockSpec((tm, tn), lambda i, j: (i, ga0 + j)),
            pl.BlockSpec((tm, tn), lambda i, j: (i, gb0 + j)),
        ],
        out_specs=pl.BlockSpec((tm, tn), lambda i, j: (i, j)),
        out_shape=jax.ShapeDtypeStruct((m, n), BF16),
        compiler_params=_cparams(("parallel", "arbitrary")),
        name="upgate",
    )(ya, yb, wa, wb, proj, proj)


def _head_mask(rows, cols):
    r = lax.broadcasted_iota(jnp.int32, (rows, cols), 0) // RWKV_HEAD
    c = lax.broadcasted_iota(jnp.int32, (rows, cols), 1) // RWKV_HEAD
    return r == c


def _split_bf16(t):
    hi = t.astype(BF16)
    lo = (t - hi.astype(F32)).astype(BF16)
    return hi, lo


def _head_sum(t, ones_bd):
    hi, lo = _split_bf16(t)
    return (jnp.dot(hi, ones_bd, preferred_element_type=F32)
            + jnp.dot(lo, ones_bd, preferred_element_type=F32))


def _prep_body(r_ref, k_ref, v_ref, lo_ref,
               rp_ref, kp_ref, vp_ref, lop_ref, rn_ref, kn_ref, vn_ref, lon_ref,
               mup_ref, mun_ref, lmup_ref, lmun_ref,
               w0f_ref, w0b_ref, a0f_ref, a0b_ref, kk_ref, ka_ref, rk_ref,
               w2f_ref, w2b_ref, a2f_ref, a2b_ref, g2_ref,
               ro_ref, vo_ref, kko_ref, kdf_ref, kdb_ref, bf_ref, bb_ref, lwf_ref, lwb_ref,
               g_ref, bonus_ref,
               tw_s, da_s, sg_s, *, tb_rows):
    tb = pl.program_id(1)
    hg = pl.program_id(2)
    first = tb == 0
    last = tb == pl.num_programs(1) - 1

    def shift_mix(x_ref, p_ref, n_ref, mup, mun):
        x = x_ref[0].astype(F32)
        width = x.shape[1]
        prev_row = jnp.where(first, 0.0, p_ref[0][7:8, :].astype(F32))
        next_row = jnp.where(last, 0.0, n_ref[0][0:1, :].astype(F32))
        row = lax.broadcasted_iota(jnp.int32, (tb_rows, width), 0)
        xp = jnp.where(row == 0, prev_row, pltpu.roll(x, 1, axis=0))
        xn = jnp.where(row == tb_rows - 1, next_row, pltpu.roll(x, tb_rows - 1, axis=0))
        return x + mup * (xp - x) + mun * (xn - x)

    @pl.when(hg == 0)
    def _():
        lo = shift_mix(lo_ref, lop_ref, lon_ref, lmup_ref[...], lmun_ref[...])
        tw_s[...] = jnp.tanh(lo[:, :RANK_LORA_PAD]).astype(BF16)
        da_s[...] = lo[:, RANK_LORA_PAD:2 * RANK_LORA_PAD].astype(BF16)
        sg_s[...] = _sigmoid(lo[:, 2 * RANK_LORA_PAD:]).astype(BF16)

    mu_p = mup_ref[...]
    mu_n = mun_ref[...]
    r = shift_mix(r_ref, rp_ref, rn_ref, mu_p[:, 0:HEAD_GROUP], mu_n[:, 0:HEAD_GROUP])
    k = shift_mix(k_ref, kp_ref, kn_ref, mu_p[:, HEAD_GROUP:2 * HEAD_GROUP], mu_n[:, HEAD_GROUP:2 * HEAD_GROUP])
    v = shift_mix(v_ref, vp_ref, vn_ref, mu_p[:, 2 * HEAD_GROUP:], mu_n[:, 2 * HEAD_GROUP:])

    ones_bd = jnp.where(_head_mask(HEAD_GROUP, HEAD_GROUP), 1.0, 0.0).astype(BF16)
    kk = k * kk_ref[...]
    kk = kk * lax.rsqrt(jnp.maximum(_head_sum(kk * kk, ones_bd), 1e-24))

    tw = tw_s[...]
    da = da_s[...]

    def direction(w0_ref, w2_ref, a0_ref, a2_ref):
        dec = w0_ref[...] + jnp.dot(tw, w2_ref[...], preferred_element_type=F32)
        z = -dec
        softplus = jnp.maximum(z, 0.0) + jnp.log1p(jnp.exp(-jnp.abs(z)))
        lw = -jnp.exp(-softplus - 0.5)
        a = _sigmoid(a0_ref[...] + jnp.dot(da, a2_ref[...], preferred_element_type=F32))
        kd = k * (1.0 + (a - 1.0) * ka_ref[...])
        return lw, a, kd

    lw_f, a_f, kd_f = direction(w0f_ref, w2f_ref, a0f_ref, a2f_ref)
    lw_b, a_b, kd_b = direction(w0b_ref, w2b_ref, a0b_ref, a2b_ref)

    bonus = _head_sum(r * (0.5 * (kd_f + kd_b)) * rk_ref[...], ones_bd) * v
    g = jnp.dot(sg_s[...], g2_ref[...], preferred_element_type=F32)

    ro_ref[0] = r.astype(ro_ref.dtype)
    vo_ref[0] = v.astype(vo_ref.dtype)
    kko_ref[0] = kk.astype(kko_ref.dtype)
    kdf_ref[0] = kd_f.astype(kdf_ref.dtype)
    kdb_ref[0] = kd_b.astype(kdb_ref.dtype)
    bf_ref[0] = (kk * a_f).astype(bf_ref.dtype)
    bb_ref[0] = (kk * a_b).astype(bb_ref.dtype)
    lwf_ref[0] = lw_f
    lwb_ref[0] = lw_b
    g_ref[0] = g.astype(g_ref.dtype)
    bonus_ref[0] = bonus.astype(bonus_ref.dtype)


def _rwkv_prep(proj3, vecs, mats, tb_rows=256):
    bsz, seq, _ = proj3.shape
    ntb = seq // tb_rows
    hgw = HEAD_GROUP
    n8 = seq // 8
    r0, k0, v0 = 0, D_RWKV // hgw, 2 * D_RWKV // hgw
    lo0 = COL_LORA // LORA_W

    def main(c0):
        return pl.BlockSpec((1, tb_rows, hgw), lambda b, t, h: (b, t, c0 + h))

    def prev(c0):
        return pl.BlockSpec((1, 8, hgw), lambda b, t, h: (b, jnp.maximum(t * (tb_rows // 8) - 1, 0), c0 + h))

    def nxt(c0):
        return pl.BlockSpec((1, 8, hgw), lambda b, t, h: (b, jnp.minimum((t + 1) * (tb_rows // 8), n8 - 1), c0 + h))

    lo_main = pl.BlockSpec((1, tb_rows, LORA_W), lambda b, t, h: (b, t, lo0))
    lo_prev = pl.BlockSpec((1, 8, LORA_W), lambda b, t, h: (b, jnp.maximum(t * (tb_rows // 8) - 1, 0), lo0))
    lo_next = pl.BlockSpec((1, 8, LORA_W), lambda b, t, h: (b, jnp.minimum((t + 1) * (tb_rows // 8), n8 - 1), lo0))

    def hvec():
        return pl.BlockSpec((1, hgw), lambda b, t, h: (0, h))

    def hmat(rows):
        return pl.BlockSpec((rows, hgw), lambda b, t, h: (0, h))

    mu3 = pl.BlockSpec((None, 1, 3 * hgw), lambda b, t, h: (h, 0, 0))
    full = lambda shape: pl.BlockSpec(shape, lambda b, t, h: (0,) * len(shape))

    in_specs = [main(r0), main(k0), main(v0), lo_main,
                prev(r0), prev(k0), prev(v0), lo_prev, nxt(r0), nxt(k0), nxt(v0), lo_next,
                mu3, mu3, full((1, LORA_W)), full((1, LORA_W)),
                hvec(), hvec(), hvec(), hvec(), hvec(), hvec(), hvec(),
                hmat(RANK_LORA_PAD), hmat(RANK_LORA_PAD), hmat(RANK_LORA_PAD), hmat(RANK_LORA_PAD),
                hmat(RANK_GATE)]
    out_block = pl.BlockSpec((1, tb_rows, hgw), lambda b, t, h: (b, t, h))
    shp = lambda dt: jax.ShapeDtypeStruct((bsz, seq, D_RWKV), dt)
    out_dtypes = [BF16, BF16, BF16, BF16, BF16, BF16, BF16, F32, F32, BF16, F32]
    return pl.pallas_call(
        functools.partial(_prep_body, tb_rows=tb_rows),
        grid=(bsz, ntb, N_HEAD_GROUPS),
        in_specs=in_specs,
        out_specs=[out_block] * len(out_dtypes),
        out_shape=[shp(dt) for dt in out_dtypes],
        scratch_shapes=[pltpu.VMEM((tb_rows, RANK_LORA_PAD), BF16),
                        pltpu.VMEM((tb_rows, RANK_LORA_PAD), BF16),
                        pltpu.VMEM((tb_rows, RANK_GATE), BF16)],
        compiler_params=_cparams(("parallel", "parallel", "arbitrary")),
        name="rwkv_prep",
    )(proj3, proj3, proj3, proj3, proj3, proj3, proj3, proj3, proj3, proj3, proj3, proj3,
      vecs["mu_prev3"], vecs["mu_next3"], vecs["lmu_prev"], vecs["lmu_next"],
      vecs["w0f"], vecs["w0b"], vecs["a0f"], vecs["a0b"], vecs["k_k"], vecs["k_a"], vecs["r_k"],
      mats["w2f"], mats["w2b"], mats["a2f"], mats["a2b"], mats["g2"])


def _dot_nt(a, b):
    return lax.dot_general(a, b, (((1,), (1,)), ((), ())), preferred_element_type=F32)


def _dot_tn(a, b):
    return lax.dot_general(a, b, (((0,), (0,)), ((), ())), preferred_element_type=F32)


def _block_diag(t, mask):
    return jnp.where(mask, jnp.concatenate([t, t, t, t], axis=0), jnp.zeros((), t.dtype))


def _diag_blocks(full, colhead):
    out = full[0:RWKV_HEAD]
    for h in range(1, HEAD_GROUP // RWKV_HEAD):
        out = jnp.where(colhead == h, full[h * RWKV_HEAD:(h + 1) * RWKV_HEAD], out)
    return out


def _scan_chunk(r, v, kk, kd, beta, lw, state, reverse):
    c = CHUNK
    bd_mask = _head_mask(HEAD_GROUP, HEAD_GROUP)
    t_idx = lax.broadcasted_iota(jnp.int32, (c, HEAD_GROUP), 0)
    s_idx = lax.broadcasted_iota(jnp.int32, (c, HEAD_GROUP), 1) % c
    colhead = lax.broadcasted_iota(jnp.int32, (c, HEAD_GROUP), 1) // RWKV_HEAD
    if reverse:
        strict, incl = s_idx > t_idx, s_idx >= t_idx
        edge = 0
    else:
        strict, incl = s_idx < t_idx, s_idx <= t_idx
        edge = c - 1
    tri = jnp.where(incl[:, :c], 1.0, 0.0).astype(BF16)

    lw_hi, lw_lo = _split_bf16(lw)
    cum = (jnp.dot(tri, lw_hi, preferred_element_type=F32)
           + jnp.dot(tri, lw_lo, preferred_element_type=F32))
    cum_edge = cum[edge:edge + 1, :]
    e_pos = jnp.exp(cum)
    e_neg = jnp.exp(-cum)
    e_edge = jnp.exp(cum_edge) * e_neg
    a_t = -kk * jnp.exp(cum - lw)
    r_t = r * e_pos
    b_t = beta * e_neg
    k_t = kd * e_neg
    b_p = beta * e_edge
    k_p = kd * e_edge

    v16 = v.astype(BF16)
    bd_v = _block_diag(v16, bd_mask)
    ar = jnp.concatenate([a_t, r_t], axis=0).astype(BF16)
    pb = _dot_nt(ar, _block_diag(b_t.astype(BF16), bd_mask))
    pk = _dot_nt(ar, _block_diag(k_t.astype(BF16), bd_mask))
    p = jnp.where(strict, pb[:c], 0.0)
    q = jnp.where(strict, pk[:c], 0.0)
    mrb = jnp.where(incl, pb[c:], 0.0)
    mrk = jnp.where(incl, pk[c:], 0.0)

    qv = jnp.dot(q.astype(BF16), bd_v, preferred_element_type=F32)

    eye = jnp.where(s_idx == t_idx, 1.0, 0.0)
    tmat = eye + p
    pw = jnp.dot(p.astype(BF16), _block_diag(p.astype(BF16), bd_mask), preferred_element_type=F32)
    levels = int(math.log2(c))
    for lev in range(1, levels):
        bd_pw = _block_diag(pw.astype(BF16), bd_mask)
        if lev < levels - 1:
            both = jnp.dot(jnp.concatenate([tmat, pw], axis=0).astype(BF16), bd_pw, preferred_element_type=F32)
            tmat = tmat + both[:c]
            pw = both[c:]
        else:
            tmat = tmat + jnp.dot(tmat.astype(BF16), bd_pw, preferred_element_type=F32)

    t16 = tmat.astype(BF16)
    a_hat = jnp.dot(t16, _block_diag(a_t.astype(BF16), bd_mask), preferred_element_type=F32)
    w_hat = jnp.dot(t16, _block_diag(qv.astype(BF16), bd_mask), preferred_element_type=F32)

    a16, w16 = a_hat.astype(BF16), w_hat.astype(BF16)
    z = jnp.concatenate([b_p, k_p], axis=0).astype(BF16)
    y_g = jnp.concatenate([a16, jnp.zeros_like(a16)], axis=0)
    y_h = jnp.concatenate([w16, v16], axis=0)
    g_full = _dot_tn(z, y_g)
    h_full = _dot_tn(z, y_h)
    n_idx = lax.broadcasted_iota(jnp.int32, (RWKV_HEAD, HEAD_GROUP), 0)
    j_idx = lax.broadcasted_iota(jnp.int32, (RWKV_HEAD, HEAD_GROUP), 1)
    colhead_n = j_idx // RWKV_HEAD
    g_mat = _diag_blocks(g_full, colhead_n) + jnp.where(n_idx == j_idx % RWKV_HEAD, jnp.exp(cum_edge), 0.0)
    h_mat = _diag_blocks(h_full, colhead_n)

    mrb16 = mrb.astype(BF16)
    r_hat = r_t + jnp.dot(mrb16, _block_diag(a16, bd_mask), preferred_element_type=F32)
    o_intra = (jnp.dot(mrb16, _block_diag(w16, bd_mask), preferred_element_type=F32)
               + jnp.dot(mrk.astype(BF16), bd_v, preferred_element_type=F32))

    bd_s = _block_diag(state.astype(BF16), bd_mask)
    rg = jnp.concatenate([r_hat, g_mat], axis=0).astype(BF16)
    both = jnp.dot(rg, bd_s, preferred_element_type=F32)
    out = both[:c] + o_intra
    new_state = both[c:] + h_mat
    return out, new_state


def _scan_body(rf_ref, vf_ref, kkf_ref, kdf_ref, bf_ref, lwf_ref,
               rb_ref, vb_ref, kkb_ref, kdb_ref, bb_ref, lwb_ref,
               of_ref, ob_ref, sf_ref, sb_ref):
    @pl.when(pl.program_id(2) == 0)
    def _():
        sf_ref[...] = jnp.zeros_like(sf_ref)
        sb_ref[...] = jnp.zeros_like(sb_ref)

    ld = lambda ref: ref[0].astype(F32)
    o_f, s_f = _scan_chunk(ld(rf_ref), ld(vf_ref), ld(kkf_ref), ld(kdf_ref), ld(bf_ref), ld(lwf_ref),
                           sf_ref[...], reverse=False)
    o_b, s_b = _scan_chunk(ld(rb_ref), ld(vb_ref), ld(kkb_ref), ld(kdb_ref), ld(bb_ref), ld(lwb_ref),
                           sb_ref[...], reverse=True)
    of_ref[0] = o_f
    ob_ref[0] = o_b
    sf_ref[...] = s_f
    sb_ref[...] = s_b


def _rwkv_scan(r, v, kk, kd_f, kd_b, beta_f, beta_b, lw_f, lw_b):
    bsz, seq, _ = r.shape
    nc = seq // CHUNK
    fwd = pl.BlockSpec((1, CHUNK, HEAD_GROUP), lambda b, h, c: (b, c, h))
    bwd = pl.BlockSpec((1, CHUNK, HEAD_GROUP), lambda b, h, c: (b, nc - 1 - c, h))
    out = jax.ShapeDtypeStruct((bsz, seq, D_RWKV), F32)
    return pl.pallas_call(
        _scan_body,
        grid=(bsz, N_HEAD_GROUPS, nc),
        in_specs=[fwd] * 6 + [bwd] * 6,
        out_specs=[fwd, bwd],
        out_shape=[out, out],
        scratch_shapes=[pltpu.VMEM((RWKV_HEAD, HEAD_GROUP), F32), pltpu.VMEM((RWKV_HEAD, HEAD_GROUP), F32)],
        compiler_params=_cparams(("parallel", "parallel", "arbitrary")),
        name="rwkv_scan",
    )(r, v, kk, kd_f, beta_f, lw_f, r, v, kk, kd_b, beta_b, lw_b)


def _post_body(of_ref, ob_ref, bonus_ref, g_ref, lng_ref, lnb_ref, y_ref):
    ones_bd = jnp.where(_head_mask(HEAD_GROUP, HEAD_GROUP), 1.0, 0.0).astype(BF16)
    o = of_ref[0] + ob_ref[0]
    inv_n = 1.0 / RWKV_HEAD
    mu = _head_sum(o, ones_bd) * inv_n
    d = o - mu
    var = _head_sum(d * d, ones_bd) * inv_n
    on = d * lax.rsqrt(var + EPS_GN) * lng_ref[...] + lnb_ref[...]
    y_ref[0] = ((on + bonus_ref[0]) * g_ref[0].astype(F32)).astype(y_ref.dtype)


def _rwkv_post(o_f, o_b, bonus, g, ln_g, ln_b, tb_rows=512):
    bsz, seq, _ = o_f.shape
    blk = pl.BlockSpec((1, tb_rows, HEAD_GROUP), lambda b, t, h: (b, t, h))
    vec = pl.BlockSpec((1, HEAD_GROUP), lambda b, t, h: (0, h))
    return pl.pallas_call(
        _post_body,
        grid=(bsz, seq // tb_rows, N_HEAD_GROUPS),
        in_specs=[blk, blk, blk, blk, vec, vec],
        out_specs=blk,
        out_shape=jax.ShapeDtypeStruct((bsz, seq, D_RWKV), BF16),
        compiler_params=_cparams(("parallel", "parallel", "parallel")),
        name="rwkv_post",
    )(o_f, o_b, bonus, g, ln_g, ln_b)


def _attn_body(q_ref, k_ref, v_ref, lq1_ref, lk1_ref, lq2_ref, lk2_ref, sg_ref, o_ref, *, tq):
    head = pl.program_id(1)
    qb = pl.program_id(2)
    seq = k_ref.shape[1]
    d = DIFF_HEAD

    lam = (jnp.exp(jnp.sum(lq1_ref[...] * lk1_ref[...], axis=-1, keepdims=True))
           - jnp.exp(jnp.sum(lq2_ref[...] * lk2_ref[...], axis=-1, keepdims=True)) + LAMBDA_INIT)
    slope = jnp.exp2(jnp.full((1, 1), -8.0 / N_DIFF_HEADS, F32) * (head + 1).astype(F32))

    q = q_ref[0] * jnp.asarray(d ** -0.5, BF16)
    lane = lax.broadcasted_iota(jnp.int32, (tq, 2 * d), 1)
    zero = jnp.zeros((), BF16)
    qq = jnp.concatenate([jnp.where(lane < d, q, zero), jnp.where(lane >= d, q, zero)], axis=0)
    s = _dot_nt(qq, k_ref[0])

    qpos = qb * tq + lax.broadcasted_iota(jnp.int32, (tq, seq), 0)
    kpos = lax.broadcasted_iota(jnp.int32, (tq, seq), 1)
    bias = -slope * jnp.abs(qpos - kpos).astype(F32)

    def softmax(t):
        t = t + bias
        e = jnp.exp(t - jnp.max(t, axis=-1, keepdims=True))
        return e, jnp.sum(e, axis=-1, keepdims=True)

    e1, l1 = softmax(s[:tq])
    e2, l2 = softmax(s[tq:])
    attn = e1 * (1.0 / l1) - e2 * (lam / l2)
    out = jnp.dot(attn.astype(BF16), v_ref[0], preferred_element_type=F32)
    out = out * lax.rsqrt(jnp.mean(out * out, axis=-1, keepdims=True) + EPS_SUBLN) * sg_ref[...]
    o_ref[0] = (out * (1.0 - LAMBDA_INIT)).astype(o_ref.dtype)


def _diff_attn(proj3, lq1, lk1, lq2, lk2, subln_g, tq=128):
    bsz, seq, _ = proj3.shape
    w = DIFF_VDIM
    q0 = COL_QKV // w
    k0 = (COL_QKV + D_DIFF) // w
    v0 = (COL_QKV + 2 * D_DIFF) // w
    vec = lambda n: pl.BlockSpec((1, n), lambda b, h, i: (0, 0))
    return pl.pallas_call(
        functools.partial(_attn_body, tq=tq),
        grid=(bsz, N_DIFF_HEADS, seq // tq),
        in_specs=[pl.BlockSpec((1, tq, w), lambda b, h, i: (b, i, q0 + h)),
                  pl.BlockSpec((1, seq, w), lambda b, h, i: (b, 0, k0 + h)),
                  pl.BlockSpec((1, seq, w), lambda b, h, i: (b, 0, v0 + h)),
                  vec(DIFF_HEAD), vec(DIFF_HEAD), vec(DIFF_HEAD), vec(DIFF_HEAD), vec(w)],
        out_specs=pl.BlockSpec((1, tq, w), lambda b, h, i: (b, i, h)),
        out_shape=jax.ShapeDtypeStruct((bsz, seq, D_DIFF), BF16),
        compiler_params=_cparams(("parallel", "parallel", "arbitrary")),
        name="diff_attn",
    )(proj3, proj3, proj3, lq1.reshape(1, -1), lk1.reshape(1, -1), lq2.reshape(1, -1), lk2.reshape(1, -1),
      subln_g.reshape(1, -1))


def _pad_cols(t, width):
    return jnp.pad(t, ((0, 0), (0, width - t.shape[1])))


def _pad_rows(t, rows):
    return jnp.pad(t, ((0, rows - t.shape[0]), (0, 0)))


def kernel(x, attn_pre_norm, attn_post_norm, w_in, shift_prev, shift_next, decay_bias_fwd, decay_up_fwd, decay_bias_bwd, decay_up_bwd, iclr_bias_fwd, iclr_up_fwd, iclr_bias_bwd, iclr_up_bwd, gate_up, k_k, k_a, r_k, ln_x_gain, ln_x_bias, lambda_q1, lambda_k1, lambda_q2, lambda_k2, subln_gain, w_up_rwkv, w_up_diff, w_out, mlp_pre_norm, mlp_post_norm, w_mlp_in, w_mlp_out):
    bsz, seq, d = x.shape
    m = bsz * seq
    l = 0
    x2 = x.reshape(m, d)

    w = w_in[l]
    c_dw = 3 * D_RWKV
    c_da = c_dw + RANK_LORA
    c_dg = c_da + RANK_LORA
    c_q = c_dg + RANK_GATE
    w_cat = jnp.concatenate([
        w[:, :c_dw],
        _pad_cols(w[:, c_dw:c_da], RANK_LORA_PAD), _pad_cols(w[:, c_da:c_dg], RANK_LORA_PAD), w[:, c_dg:c_q],
        w[:, c_q:]], axis=1).astype(BF16)

    def regroup(vec):
        t = vec.reshape(3, N_HEAD_GROUPS, HEAD_GROUP)
        return jnp.transpose(t, (1, 0, 2)).reshape(N_HEAD_GROUPS, 1, 3 * HEAD_GROUP)

    def lora_vec(vec):
        t = vec.reshape(1, -1)
        return jnp.concatenate([_pad_cols(t[:, c_dw:c_da], RANK_LORA_PAD), _pad_cols(t[:, c_da:c_dg], RANK_LORA_PAD),
                                t[:, c_dg:c_q]], axis=1)

    row = lambda t: t.reshape(1, -1)
    vecs = dict(
        mu_prev3=regroup(shift_prev[l][:c_dw]), mu_next3=regroup(shift_next[l][:c_dw]),
        lmu_prev=lora_vec(shift_prev[l]), lmu_next=lora_vec(shift_next[l]),
        w0f=row(decay_bias_fwd[l]), w0b=row(decay_bias_bwd[l]), a0f=row(iclr_bias_fwd[l]), a0b=row(iclr_bias_bwd[l]),
        k_k=row(k_k[l]), k_a=row(k_a[l]), r_k=row(r_k[l]))
    mats = dict(
        w2f=_pad_rows(decay_up_fwd[l], RANK_LORA_PAD).astype(BF16), w2b=_pad_rows(decay_up_bwd[l], RANK_LORA_PAD).astype(BF16),
        a2f=_pad_rows(iclr_up_fwd[l], RANK_LORA_PAD).astype(BF16), a2b=_pad_rows(iclr_up_bwd[l], RANK_LORA_PAD).astype(BF16),
        g2=gate_up[l].astype(BF16))

    h = _prenorm(x2, attn_pre_norm[l])
    proj = _matmul(h, w_cat, BF16, tm=1024, tn=512, name="in_proj")
    proj3 = proj.reshape(bsz, seq, N_PROJ)

    r, v, kk, kd_f, kd_b, beta_f, beta_b, lw_f, lw_b, g, bonus = _rwkv_prep(proj3, vecs, mats)
    o_f, o_b = _rwkv_scan(r, v, kk, kd_f, kd_b, beta_f, beta_b, lw_f, lw_b)
    y_a = _rwkv_post(o_f, o_b, bonus, g, row(ln_x_gain[l]), row(ln_x_bias[l]))

    y_b = _diff_attn(proj3, lambda_q1[l], lambda_k1[l], lambda_q2[l], lambda_k2[l], subln_gain[l])

    mixed = _upgate(y_a.reshape(m, D_RWKV), y_b.reshape(m, D_DIFF),
                    w_up_rwkv[l].astype(BF16), w_up_diff[l].astype(BF16), proj)
    z = _matmul(mixed, w_out[l].astype(BF16), F32, tm=1024, tn=512, name="out_proj")
    x1, h2 = _resnorm(x2, z, attn_post_norm[l], mlp_pre_norm[l])

    u = _matmul(h2, w_mlp_in[l].astype(BF16), BF16, tm=1024, tn=512, act="relu2", name="mlp_in")
    z2 = _matmul_ktiled(u, w_mlp_out[l].astype(BF16), F32, tm=1024, tn=1024, tk=2048, name="mlp_out")
    out = _resnorm(x1, z2, mlp_post_norm[l])
    return out.reshape(bsz, seq, d)
```

```python
import functools
import math

import jax
import jax.numpy as jnp
from jax import lax
from jax.experimental import pallas as pl
from jax.experimental.pallas import tpu as pltpu

F32 = jnp.float32
BF16 = jnp.bfloat16

D_MODEL = 4096
D_RWKV = D_MODEL // 2
RWKV_HEAD = 64
RANK_LORA = 96
RANK_LORA_PAD = 128
RANK_GATE = 256
D_DIFF = D_MODEL // 2
DIFF_HEAD = 64
N_DIFF_HEADS = D_DIFF // (2 * DIFF_HEAD)
DIFF_VDIM = 2 * DIFF_HEAD
D_FF = 4 * D_MODEL
EPS_RMS = 1e-6
EPS_GN = 64e-5
EPS_SUBLN = 1e-5
LAMBDA_INIT = 0.8 - 0.6 * math.exp(-0.3 * 0)
LOG2E = math.log2(math.e)

HEAD_GROUP = 256
N_HEAD_GROUPS = D_RWKV // HEAD_GROUP
LORA_W = 2 * RANK_LORA_PAD + RANK_GATE
COL_GATE = 3 * D_DIFF
N_QG = COL_GATE + 2 * D_MODEL

CHUNK = 64
SCAN_GROUPS = 4
VMEM_LIMIT = 48 * 1024 * 1024


def _cparams(sem):
    return pltpu.CompilerParams(dimension_semantics=sem, vmem_limit_bytes=VMEM_LIMIT)


def _rms(t, g):
    return t * lax.rsqrt(jnp.mean(t * t, axis=-1, keepdims=True) + EPS_RMS) * g


def _prenorm_body(x_ref, g_ref, o_ref):
    o_ref[...] = _rms(x_ref[...], g_ref[...]).astype(o_ref.dtype)


def _prenorm(x, g, tr=256):
    m, d = x.shape
    return pl.pallas_call(
        _prenorm_body,
        grid=(m // tr,),
        in_specs=[pl.BlockSpec((tr, d), lambda i: (i, 0)), pl.BlockSpec((1, d), lambda i: (0, 0))],
        out_specs=pl.BlockSpec((tr, d), lambda i: (i, 0)),
        out_shape=jax.ShapeDtypeStruct((m, d), BF16),
        compiler_params=_cparams(("parallel",)),
        name="prenorm",
    )(x, g.reshape(1, d))


def _resnorm_body(x_ref, z_ref, g_ref, g2_ref, o_ref, h_ref):
    y = x_ref[...] + _rms(z_ref[...], g_ref[...])
    o_ref[...] = y
    if h_ref is not None:
        h_ref[...] = _rms(y, g2_ref[...]).astype(h_ref.dtype)


def _resnorm(x, z, g, g_next=None, tr=256):
    m, d = x.shape
    row = pl.BlockSpec((tr, d), lambda i: (i, 0))
    vec = pl.BlockSpec((1, d), lambda i: (0, 0))
    if g_next is None:
        body = lambda x_ref, z_ref, g_ref, o_ref: _resnorm_body(x_ref, z_ref, g_ref, None, o_ref, None)
        return pl.pallas_call(
            body, grid=(m // tr,), in_specs=[row, row, vec], out_specs=row,
            out_shape=jax.ShapeDtypeStruct((m, d), F32),
            compiler_params=_cparams(("parallel",)), name="resnorm_out",
        )(x, z, g.reshape(1, d))
    return pl.pallas_call(
        _resnorm_body, grid=(m // tr,), in_specs=[row, row, vec, vec], out_specs=[row, row],
        out_shape=[jax.ShapeDtypeStruct((m, d), F32), jax.ShapeDtypeStruct((m, d), BF16)],
        compiler_params=_cparams(("parallel",)), name="resnorm_mid",
    )(x, z, g.reshape(1, d), g_next.reshape(1, d))


def _mm_body(a_ref, b_ref, o_ref, *, act):
    acc = jnp.dot(a_ref[...], b_ref[...], preferred_element_type=F32)
    if act == "relu2":
        acc = jnp.square(jnp.maximum(acc, 0.0))
    o_ref[...] = acc.astype(o_ref.dtype)


def _mm_scaled_body(a_ref, b_ref, s_ref, o_ref):
    acc = jnp.dot(a_ref[...], b_ref[...], preferred_element_type=F32)
    o_ref[...] = (acc * s_ref[...]).astype(o_ref.dtype)


def _matmul(a, b, out_dtype, tm, tn, act=None, col_scale=None, name="matmul"):
    m, k = a.shape
    _, n = b.shape
    in_specs = [pl.BlockSpec((tm, k), lambda i, j: (i, 0)), pl.BlockSpec((k, tn), lambda i, j: (0, j))]
    args = (a, b)
    body = functools.partial(_mm_body, act=act)
    if col_scale is not None:
        assert act is None
        in_specs.append(pl.BlockSpec((1, tn), lambda i, j: (0, j)))
        args = (a, b, col_scale)
        body = _mm_scaled_body
    return pl.pallas_call(
        body,
        grid=(m // tm, n // tn),
        in_specs=in_specs,
        out_specs=pl.BlockSpec((tm, tn), lambda i, j: (i, j)),
        out_shape=jax.ShapeDtypeStruct((m, n), out_dtype),
        compiler_params=_cparams(("parallel", "arbitrary")),
        name=name,
    )(*args)


def _mmk_body(a_ref, b_ref, o_ref, acc_ref):
    kk = pl.program_id(2)

    @pl.when(kk == 0)
    def _():
        acc_ref[...] = jnp.zeros_like(acc_ref)

    acc_ref[...] += jnp.dot(a_ref[...], b_ref[...], preferred_element_type=F32)

    @pl.when(kk == pl.num_programs(2) - 1)
    def _():
        o_ref[...] = acc_ref[...].astype(o_ref.dtype)


def _matmul_ktiled(a, b, out_dtype, tm, tn, tk, name="matmul_k"):
    m, k = a.shape
    _, n = b.shape
    return pl.pallas_call(
        _mmk_body,
        grid=(m // tm, n // tn, k // tk),
        in_specs=[pl.BlockSpec((tm, tk), lambda i, j, q: (i, q)), pl.BlockSpec((tk, tn), lambda i, j, q: (q, j))],
        out_specs=pl.BlockSpec((tm, tn), lambda i, j, q: (i, j)),
        out_shape=jax.ShapeDtypeStruct((m, n), out_dtype),
        scratch_shapes=[pltpu.VMEM((tm, tn), F32)],
        compiler_params=_cparams(("parallel", "parallel", "arbitrary")),
        name=name,
    )(a, b)


def _sigmoid(t):
    return 1.0 / (1.0 + jnp.exp(-t))


def _upgate_body(ya_ref, yb_ref, wa_ref, wb_ref, ga_ref, gb_ref, o_ref):
    acc_a = jnp.dot(ya_ref[...], wa_ref[...], preferred_element_type=F32)
    acc_b = jnp.dot(yb_ref[...], wb_ref[...], preferred_element_type=F32)
    ga = _sigmoid(ga_ref[...].astype(F32))
    gb = _sigmoid(gb_ref[...].astype(F32))
    o_ref[...] = (ga * acc_a + gb * acc_b).astype(o_ref.dtype)


def _upgate(ya, yb, wa, wb, qg, tm=1024, tn=512):
    m, k = ya.shape
    n = wa.shape[1]
    ga0 = COL_GATE // tn
    gb0 = (COL_GATE + D_MODEL) // tn
    return pl.pallas_call(
        _upgate_body,
        grid=(m // tm, n // tn),
        in_specs=[
            pl.BlockSpec((tm, k), lambda i, j: (i, 0)),
            pl.BlockSpec((tm, k), lambda i, j: (i, 0)),
            pl.BlockSpec((k, tn), lambda i, j: (0, j)),
            pl.BlockSpec((k, tn), lambda i, j: (0, j)),
            pl.BlockSpec((tm, tn), lambda i, j: (i, ga0 + j)),
            pl.BlockSpec((tm, tn), lambda i, j: (i, gb0 + j)),
        ],
        out_specs=pl.BlockSpec((tm, tn), lambda i, j: (i, j)),
        out_shape=jax.ShapeDtypeStruct((m, n), BF16),
        compiler_params=_cparams(("parallel", "arbitrary")),
        name="upgate",
    )(ya, yb, wa, wb, qg, qg)


def _head_mask(rows, cols):
    r = lax.broadcasted_iota(jnp.int32, (rows, cols), 0) // RWKV_HEAD
    c = lax.broadcasted_iota(jnp.int32, (rows, cols), 1) // RWKV_HEAD
    return r == c


def _split_bf16(t):
    hi = t.astype(BF16)
    lo = (t - hi.astype(F32)).astype(BF16)
    return hi, lo


def _head_sum(t, ones_bd):
    hi, lo = _split_bf16(t)
    return (jnp.dot(hi, ones_bd, preferred_element_type=F32)
            + jnp.dot(lo, ones_bd, preferred_element_type=F32))


def _prep_body(r_ref, k_ref, v_ref, lo_ref,
               rp_ref, kp_ref, vp_ref, lop_ref, rn_ref, kn_ref, vn_ref, lon_ref,
               mup_ref, mun_ref, lmup_ref, lmun_ref,
               w0f_ref, w0b_ref, a0f_ref, a0b_ref, kk_ref, ka_ref, rk_ref,
               w2f_ref, w2b_ref, a2f_ref, a2b_ref, g2_ref,
               ro_ref, vo_ref, kko_ref, kdf_ref, kdb_ref, bf_ref, bb_ref, lwf_ref, lwb_ref,
               g_ref, bonus_ref,
               tw_s, da_s, sg_s, *, tb_rows):
    tb = pl.program_id(1)
    hg = pl.program_id(2)
    first = tb == 0
    last = tb == pl.num_programs(1) - 1

    def shift_mix(x_ref, p_ref, n_ref, mup, mun):
        x = x_ref[0].astype(F32)
        width = x.shape[1]
        prev_row = jnp.where(first, 0.0, p_ref[0][7:8, :].astype(F32))
        next_row = jnp.where(last, 0.0, n_ref[0][0:1, :].astype(F32))
        row = lax.broadcasted_iota(jnp.int32, (tb_rows, width), 0)
        xp = jnp.where(row == 0, prev_row, pltpu.roll(x, 1, axis=0))
        xn = jnp.where(row == tb_rows - 1, next_row, pltpu.roll(x, tb_rows - 1, axis=0))
        return x + mup * (xp - x) + mun * (xn - x)

    @pl.when(hg == 0)
    def _():
        lo = shift_mix(lo_ref, lop_ref, lon_ref, lmup_ref[...], lmun_ref[...])
        tw_s[...] = jnp.tanh(lo[:, :RANK_LORA_PAD]).astype(BF16)
        da_s[...] = lo[:, RANK_LORA_PAD:2 * RANK_LORA_PAD].astype(BF16)
        sg_s[...] = _sigmoid(lo[:, 2 * RANK_LORA_PAD:]).astype(BF16)

    mu_p = mup_ref[...]
    mu_n = mun_ref[...]
    r = shift_mix(r_ref, rp_ref, rn_ref, mu_p[:, 0:HEAD_GROUP], mu_n[:, 0:HEAD_GROUP])
    k = shift_mix(k_ref, kp_ref, kn_ref, mu_p[:, HEAD_GROUP:2 * HEAD_GROUP], mu_n[:, HEAD_GROUP:2 * HEAD_GROUP])
    v = shift_mix(v_ref, vp_ref, vn_ref, mu_p[:, 2 * HEAD_GROUP:], mu_n[:, 2 * HEAD_GROUP:])

    ones_bd = jnp.where(_head_mask(HEAD_GROUP, HEAD_GROUP), 1.0, 0.0).astype(BF16)
    kk = k * kk_ref[...]
    kk = kk * lax.rsqrt(jnp.maximum(_head_sum(kk * kk, ones_bd), 1e-24))

    tw = tw_s[...]
    da = da_s[...]

    def direction(w0_ref, w2_ref, a0_ref, a2_ref):
        dec = w0_ref[...] + jnp.dot(tw, w2_ref[...], preferred_element_type=F32)
        z = -dec
        softplus = jnp.maximum(z, 0.0) + jnp.log1p(jnp.exp(-jnp.abs(z)))
        lw = -jnp.exp(-softplus - 0.5)
        a = _sigmoid(a0_ref[...] + jnp.dot(da, a2_ref[...], preferred_element_type=F32))
        kd = k * (1.0 + (a - 1.0) * ka_ref[...])
        return lw, a, kd

    lw_f, a_f, kd_f = direction(w0f_ref, w2f_ref, a0f_ref, a2f_ref)
    lw_b, a_b, kd_b = direction(w0b_ref, w2b_ref, a0b_ref, a2b_ref)

    bonus = _head_sum(r * (0.5 * (kd_f + kd_b)) * rk_ref[...], ones_bd) * v
    g = jnp.dot(sg_s[...], g2_ref[...], preferred_element_type=F32)

    ro_ref[0] = r.astype(ro_ref.dtype)
    vo_ref[0] = v.astype(vo_ref.dtype)
    kko_ref[0] = kk.astype(kko_ref.dtype)
    kdf_ref[0] = kd_f.astype(kdf_ref.dtype)
    kdb_ref[0] = kd_b.astype(kdb_ref.dtype)
    bf_ref[0] = (kk * a_f).astype(bf_ref.dtype)
    bb_ref[0] = (kk * a_b).astype(bb_ref.dtype)
    lwf_ref[0] = lw_f
    lwb_ref[0] = lw_b
    g_ref[0] = g.astype(g_ref.dtype)
    bonus_ref[0] = bonus.astype(bonus_ref.dtype)


def _rwkv_prep(rkv3, lora3, vecs, mats, tb_rows=256):
    bsz, seq, _ = rkv3.shape
    ntb = seq // tb_rows
    hgw = HEAD_GROUP
    n8 = seq // 8
    r0, k0, v0 = 0, D_RWKV // hgw, 2 * D_RWKV // hgw
    lo0 = 0

    def main(c0):
        return pl.BlockSpec((1, tb_rows, hgw), lambda b, t, h: (b, t, c0 + h))

    def prev(c0):
        return pl.BlockSpec((1, 8, hgw), lambda b, t, h: (b, jnp.maximum(t * (tb_rows // 8) - 1, 0), c0 + h))

    def nxt(c0):
        return pl.BlockSpec((1, 8, hgw), lambda b, t, h: (b, jnp.minimum((t + 1) * (tb_rows // 8), n8 - 1), c0 + h))

    lo_main = pl.BlockSpec((1, tb_rows, LORA_W), lambda b, t, h: (b, t, lo0))
    lo_prev = pl.BlockSpec((1, 8, LORA_W), lambda b, t, h: (b, jnp.maximum(t * (tb_rows // 8) - 1, 0), lo0))
    lo_next = pl.BlockSpec((1, 8, LORA_W), lambda b, t, h: (b, jnp.minimum((t + 1) * (tb_rows // 8), n8 - 1), lo0))

    def hvec():
        return pl.BlockSpec((1, hgw), lambda b, t, h: (0, h))

    def hmat(rows):
        return pl.BlockSpec((rows, hgw), lambda b, t, h: (0, h))

    mu3 = pl.BlockSpec((None, 1, 3 * hgw), lambda b, t, h: (h, 0, 0))
    full = lambda shape: pl.BlockSpec(shape, lambda b, t, h: (0,) * len(shape))

    in_specs = [main(r0), main(k0), main(v0), lo_main,
                prev(r0), prev(k0), prev(v0), lo_prev, nxt(r0), nxt(k0), nxt(v0), lo_next,
                mu3, mu3, full((1, LORA_W)), full((1, LORA_W)),
                hvec(), hvec(), hvec(), hvec(), hvec(), hvec(), hvec(),
                hmat(RANK_LORA_PAD), hmat(RANK_LORA_PAD), hmat(RANK_LORA_PAD), hmat(RANK_LORA_PAD),
                hmat(RANK_GATE)]
    out_block = pl.BlockSpec((1, tb_rows, hgw), lambda b, t, h: (b, t, h))
    shp = lambda dt: jax.ShapeDtypeStruct((bsz, seq, D_RWKV), dt)
    out_dtypes = [BF16, BF16, BF16, BF16, BF16, BF16, BF16, F32, F32, BF16, F32]
    return pl.pallas_call(
        functools.partial(_prep_body, tb_rows=tb_rows),
        grid=(bsz, ntb, N_HEAD_GROUPS),
        in_specs=in_specs,
        out_specs=[out_block] * len(out_dtypes),
        out_shape=[shp(dt) for dt in out_dtypes],
        scratch_shapes=[pltpu.VMEM((tb_rows, RANK_LORA_PAD), BF16),
                        pltpu.VMEM((tb_rows, RANK_LORA_PAD), BF16),
                        pltpu.VMEM((tb_rows, RANK_GATE), BF16)],
        compiler_params=_cparams(("parallel", "parallel", "arbitrary")),
        name="rwkv_prep",
    )(rkv3, rkv3, rkv3, lora3, rkv3, rkv3, rkv3, lora3, rkv3, rkv3, rkv3, lora3,
      vecs["mu_prev3"], vecs["mu_next3"], vecs["lmu_prev"], vecs["lmu_next"],
      vecs["w0f"], vecs["w0b"], vecs["a0f"], vecs["a0b"], vecs["k_k"], vecs["k_a"], vecs["r_k"],
      mats["w2f"], mats["w2b"], mats["a2f"], mats["a2b"], mats["g2"])


def _dot_nt(a, b):
    return lax.dot_general(a, b, (((1,), (1,)), ((), ())), preferred_element_type=F32)


def _dot_tn(a, b):
    return lax.dot_general(a, b, (((0,), (0,)), ((), ())), preferred_element_type=F32)


def _block_diag(t, mask):
    return jnp.where(mask, jnp.concatenate([t, t, t, t], axis=0), jnp.zeros((), t.dtype))


def _diag_blocks(full, colhead):
    out = full[0:RWKV_HEAD]
    for h in range(1, HEAD_GROUP // RWKV_HEAD):
        out = jnp.where(colhead == h, full[h * RWKV_HEAD:(h + 1) * RWKV_HEAD], out)
    return out


def _scan_masks():
    c = CHUNK
    t_idx = lax.broadcasted_iota(jnp.int32, (c, HEAD_GROUP), 0)
    s_idx = lax.broadcasted_iota(jnp.int32, (c, HEAD_GROUP), 1) % c
    n_idx = lax.broadcasted_iota(jnp.int32, (RWKV_HEAD, HEAD_GROUP), 0)
    j_idx = lax.broadcasted_iota(jnp.int32, (RWKV_HEAD, HEAD_GROUP), 1)
    return dict(
        bd=_head_mask(HEAD_GROUP, HEAD_GROUP),
        before={False: s_idx < t_idx, True: s_idx > t_idx},
        upto={False: s_idx <= t_idx, True: s_idx >= t_idx},
        eye=jnp.where(s_idx == t_idx, 1.0, 0.0),
        colhead=j_idx // RWKV_HEAD,
        diag=n_idx == j_idx % RWKV_HEAD)


def _scan_chunks(chains, mk):
    c = CHUNK
    bd_mask = mk["bd"]
    n = len(chains)
    rev = [ch["reverse"] for ch in chains]
    strict = [mk["before"][x] for x in rev]
    incl = [mk["upto"][x] for x in rev]
    bd = lambda t: _block_diag(t, bd_mask)
    mm = lambda a, b: jnp.dot(a, b, preferred_element_type=F32)
    each = lambda f, *cols: [f(*xs) for xs in zip(*cols)]

    tri = {x: jnp.where(mk["upto"][x][:, :c], 1.0, 0.0).astype(BF16) for x in set(rev)}
    lw = [ch["lw"] for ch in chains]
    split = each(_split_bf16, lw)
    cum2 = [mm(tri[x], jnp.concatenate([hi, lo], axis=1)) for x, (hi, lo) in zip(rev, split)]
    cum = [t[:, :HEAD_GROUP] + t[:, HEAD_GROUP:] for t in cum2]
    cum_edge = [t[0:1, :] if x else t[c - 1:c, :] for x, t in zip(rev, cum)]
    e_pos = each(jnp.exp, cum)
    e_neg = [jnp.exp(-t) for t in cum]
    g_edge = each(jnp.exp, cum_edge)
    e_edge = each(lambda g, e: g * e, g_edge, e_neg)
    a_t16 = [(-ch["kk"] * jnp.exp(t - l)).astype(BF16) for ch, t, l in zip(chains, cum, lw)]
    r_t = [ch["r"] * e for ch, e in zip(chains, e_pos)]
    b_t16 = [(ch["beta"] * e).astype(BF16) for ch, e in zip(chains, e_neg)]
    k_t16 = [(ch["kd"] * e).astype(BF16) for ch, e in zip(chains, e_neg)]
    b_p16 = [(ch["beta"] * e).astype(BF16) for ch, e in zip(chains, e_edge)]
    k_p16 = [(ch["kd"] * e).astype(BF16) for ch, e in zip(chains, e_edge)]
    v16 = [ch["v"].astype(BF16) for ch in chains]
    bd_v = each(bd, v16)

    ar = [jnp.concatenate([a, r.astype(BF16)], axis=0) for a, r in zip(a_t16, r_t)]
    pb = [_dot_nt(x, bd(b)) for x, b in zip(ar, b_t16)]
    pk = [_dot_nt(x, bd(k)) for x, k in zip(ar, k_t16)]
    p = [jnp.where(m, t[:c], 0.0) for m, t in zip(strict, pb)]
    q16 = [jnp.where(m, t[:c], 0.0).astype(BF16) for m, t in zip(strict, pk)]
    mrb16 = [jnp.where(m, t[c:], 0.0).astype(BF16) for m, t in zip(incl, pb)]
    mrk16 = [jnp.where(m, t[c:], 0.0).astype(BF16) for m, t in zip(incl, pk)]

    qv16 = [mm(a, b).astype(BF16) for a, b in zip(q16, bd_v)]

    tmat = [mk["eye"] + t for t in p]
    p16 = [t.astype(BF16) for t in p]
    pw = [mm(t, bd(t)) for t in p16]
    levels = int(math.log2(c))
    for lev in range(1, levels):
        bd_pw = [bd(t.astype(BF16)) for t in pw]
        if lev < levels - 1:
            both = [mm(jnp.concatenate([t, w], axis=0).astype(BF16), b) for t, w, b in zip(tmat, pw, bd_pw)]
            tmat = [t + x[:c] for t, x in zip(tmat, both)]
            pw = [x[c:] for x in both]
        else:
            tmat = [t + mm(t.astype(BF16), b) for t, b in zip(tmat, bd_pw)]

    t16 = [t.astype(BF16) for t in tmat]
    a16 = [mm(t, bd(a)).astype(BF16) for t, a in zip(t16, a_t16)]
    w16 = [mm(t, bd(x)).astype(BF16) for t, x in zip(t16, qv16)]

    g_full = [_dot_tn(b, a) for b, a in zip(b_p16, a16)]
    h_full = [_dot_tn(jnp.concatenate([b, k], axis=0), jnp.concatenate([w, v], axis=0))
              for b, k, w, v in zip(b_p16, k_p16, w16, v16)]
    g_mat = [_diag_blocks(t, mk["colhead"]) + jnp.where(mk["diag"], g, 0.0) for t, g in zip(g_full, g_edge)]
    h_mat = [_diag_blocks(t, mk["colhead"]) for t in h_full]

    r_hat = [r + mm(m, bd(a)) for r, m, a in zip(r_t, mrb16, a16)]
    o_intra = [mm(m, bd(w)) + mm(mk_, bv) for m, w, mk_, bv in zip(mrb16, w16, mrk16, bd_v)]

    both = [mm(jnp.concatenate([r, g], axis=0).astype(BF16), bd(ch["state"].astype(BF16)))
            for r, g, ch in zip(r_hat, g_mat, chains)]
    outs = [x[:c] + o for x, o in zip(both, o_intra)]
    states = [x[c:] + h for x, h in zip(both, h_mat)]
    return outs, states


def _scan_body(rf_ref, vf_ref, kkf_ref, kdf_ref, bf_ref, lwf_ref,
               rb_ref, vb_ref, kkb_ref, kdb_ref, bb_ref, lwb_ref,
               of_ref, ob_ref, sf_ref, sb_ref, *, groups):
    @pl.when(pl.program_id(2) == 0)
    def _():
        sf_ref[...] = jnp.zeros_like(sf_ref)
        sb_ref[...] = jnp.zeros_like(sb_ref)

    mk = _scan_masks()
    chains, sinks = [], []
    for gi in range(groups):
        cols = slice(gi * HEAD_GROUP, (gi + 1) * HEAD_GROUP)
        ld = lambda ref: ref[0, :, cols].astype(F32)
        chains.append(dict(r=ld(rf_ref), v=ld(vf_ref), kk=ld(kkf_ref), kd=ld(kdf_ref), beta=ld(bf_ref),
                           lw=ld(lwf_ref), state=sf_ref[:, cols], reverse=False))
        sinks.append((of_ref, sf_ref, cols))
        chains.append(dict(r=ld(rb_ref), v=ld(vb_ref), kk=ld(kkb_ref), kd=ld(kdb_ref), beta=ld(bb_ref),
                           lw=ld(lwb_ref), state=sb_ref[:, cols], reverse=True))
        sinks.append((ob_ref, sb_ref, cols))
    outs, states = _scan_chunks(chains, mk)
    for (o_ref, s_ref, cols), o, s in zip(sinks, outs, states):
        o_ref[0, :, cols] = o
        s_ref[:, cols] = s


def _rwkv_scan(r, v, kk, kd_f, kd_b, beta_f, beta_b, lw_f, lw_b, groups=SCAN_GROUPS):
    bsz, seq, _ = r.shape
    nc = seq // CHUNK
    width = groups * HEAD_GROUP
    fwd = pl.BlockSpec((1, CHUNK, width), lambda b, h, c: (b, c, h))
    bwd = pl.BlockSpec((1, CHUNK, width), lambda b, h, c: (b, nc - 1 - c, h))
    out = jax.ShapeDtypeStruct((bsz, seq, D_RWKV), F32)
    return pl.pallas_call(
        functools.partial(_scan_body, groups=groups),
        grid=(bsz, D_RWKV // width, nc),
        in_specs=[fwd] * 6 + [bwd] * 6,
        out_specs=[fwd, bwd],
        out_shape=[out, out],
        scratch_shapes=[pltpu.VMEM((RWKV_HEAD, width), F32), pltpu.VMEM((RWKV_HEAD, width), F32)],
        compiler_params=_cparams(("parallel", "parallel", "arbitrary")),
        name="rwkv_scan",
    )(r, v, kk, kd_f, beta_f, lw_f, r, v, kk, kd_b, beta_b, lw_b)


def _post_body(of_ref, ob_ref, bonus_ref, g_ref, lng_ref, lnb_ref, y_ref):
    ones_bd = jnp.where(_head_mask(HEAD_GROUP, HEAD_GROUP), 1.0, 0.0).astype(BF16)
    o = of_ref[0] + ob_ref[0]
    inv_n = 1.0 / RWKV_HEAD
    mu = _head_sum(o, ones_bd) * inv_n
    d = o - mu
    var = _head_sum(d * d, ones_bd) * inv_n
    on = d * lax.rsqrt(var + EPS_GN) * lng_ref[...] + lnb_ref[...]
    y_ref[0] = ((on + bonus_ref[0]) * g_ref[0].astype(F32)).astype(y_ref.dtype)


def _rwkv_post(o_f, o_b, bonus, g, ln_g, ln_b, tb_rows=512):
    bsz, seq, _ = o_f.shape
    blk = pl.BlockSpec((1, tb_rows, HEAD_GROUP), lambda b, t, h: (b, t, h))
    vec = pl.BlockSpec((1, HEAD_GROUP), lambda b, t, h: (0, h))
    return pl.pallas_call(
        _post_body,
        grid=(bsz, seq // tb_rows, N_HEAD_GROUPS),
        in_specs=[blk, blk, blk, blk, vec, vec],
        out_specs=blk,
        out_shape=jax.ShapeDtypeStruct((bsz, seq, D_RWKV), BF16),
        compiler_params=_cparams(("parallel", "parallel", "parallel")),
        name="rwkv_post",
    )(o_f, o_b, bonus, g, ln_g, ln_b)


def _attn_body(q_ref, k_ref, v_ref, lq1_ref, lk1_ref, lq2_ref, lk2_ref, sg_ref, o_ref, *, tq):
    head = pl.program_id(1)
    qb = pl.program_id(2)
    seq = k_ref.shape[1]
    d = DIFF_HEAD

    lam = (jnp.exp(jnp.sum(lq1_ref[...] * lk1_ref[...], axis=-1, keepdims=True))
           - jnp.exp(jnp.sum(lq2_ref[...] * lk2_ref[...], axis=-1, keepdims=True)) + LAMBDA_INIT)

    q = q_ref[0]
    lane = lax.broadcasted_iota(jnp.int32, (tq, 2 * d), 1)
    zero = jnp.zeros((), BF16)
    qq = jnp.concatenate([jnp.where(lane < d, q, zero), jnp.where(lane >= d, q, zero)], axis=0)

    s = _dot_nt(qq, k_ref[0])
    slope = jnp.exp2(jnp.full((1, 1), -8.0 / N_DIFF_HEADS, F32) * (head + 1).astype(F32)) * LOG2E
    qpos = qb * tq + lax.broadcasted_iota(jnp.int32, (tq, seq), 0)
    kpos = lax.broadcasted_iota(jnp.int32, (tq, seq), 1)
    bias = -slope * jnp.abs(qpos - kpos).astype(F32)

    def softmax(t):
        t = t + bias
        e = jnp.exp2(t - jnp.max(t, axis=-1, keepdims=True))
        return e, jnp.sum(e, axis=-1, keepdims=True)

    e1, l1 = softmax(s[:tq])
    e2, l2 = softmax(s[tq:])
    attn = e1 * (1.0 / l1) - e2 * (lam / l2)
    out = jnp.dot(attn.astype(BF16), v_ref[0], preferred_element_type=F32)
    out = out * lax.rsqrt(jnp.mean(out * out, axis=-1, keepdims=True) + EPS_SUBLN) * sg_ref[...]
    o_ref[0] = (out * (1.0 - LAMBDA_INIT)).astype(o_ref.dtype)


def _diff_attn(qg3, lq1, lk1, lq2, lk2, subln_g, tq=128):
    bsz, seq, _ = qg3.shape
    w = DIFF_VDIM
    q0 = 0
    k0 = D_DIFF // w
    v0 = 2 * D_DIFF // w
    vec = lambda n: pl.BlockSpec((1, n), lambda b, h, i: (0, 0))
    return pl.pallas_call(
        functools.partial(_attn_body, tq=tq),
        grid=(bsz, N_DIFF_HEADS, seq // tq),
        in_specs=[pl.BlockSpec((1, tq, w), lambda b, h, i: (b, i, q0 + h)),
                  pl.BlockSpec((1, seq, w), lambda b, h, i: (b, 0, k0 + h)),
                  pl.BlockSpec((1, seq, w), lambda b, h, i: (b, 0, v0 + h)),
                  vec(DIFF_HEAD), vec(DIFF_HEAD), vec(DIFF_HEAD), vec(DIFF_HEAD), vec(w)],
        out_specs=pl.BlockSpec((1, tq, w), lambda b, h, i: (b, i, h)),
        out_shape=jax.ShapeDtypeStruct((bsz, seq, D_DIFF), BF16),
        compiler_params=_cparams(("parallel", "parallel", "arbitrary")),
        name="diff_attn",
    )(qg3, qg3, qg3, lq1.reshape(1, -1), lk1.reshape(1, -1), lq2.reshape(1, -1), lk2.reshape(1, -1),
      subln_g.reshape(1, -1))


def _pad_cols(t, width):
    return jnp.pad(t, ((0, 0), (0, width - t.shape[1])))


def _pad_rows(t, rows):
    return jnp.pad(t, ((0, rows - t.shape[0]), (0, 0)))


def kernel(x, attn_pre_norm, attn_post_norm, w_in, shift_prev, shift_next, decay_bias_fwd, decay_up_fwd, decay_bias_bwd, decay_up_bwd, iclr_bias_fwd, iclr_up_fwd, iclr_bias_bwd, iclr_up_bwd, gate_up, k_k, k_a, r_k, ln_x_gain, ln_x_bias, lambda_q1, lambda_k1, lambda_q2, lambda_k2, subln_gain, w_up_rwkv, w_up_diff, w_out, mlp_pre_norm, mlp_post_norm, w_mlp_in, w_mlp_out):
    bsz, seq, d = x.shape
    m = bsz * seq
    l = 0
    x2 = x.reshape(m, d)

    w = w_in[l]
    c_dw = 3 * D_RWKV
    c_da = c_dw + RANK_LORA
    c_dg = c_da + RANK_LORA
    c_q = c_dg + RANK_GATE
    w_rkv = w[:, :c_dw].astype(BF16)
    w_lora = jnp.concatenate([_pad_cols(w[:, c_dw:c_da], RANK_LORA_PAD), _pad_cols(w[:, c_da:c_dg], RANK_LORA_PAD),
                              w[:, c_dg:c_q]], axis=1).astype(BF16)
    w_qg = w[:, c_q:].astype(BF16)

    def regroup(vec):
        t = vec.reshape(3, N_HEAD_GROUPS, HEAD_GROUP)
        return jnp.transpose(t, (1, 0, 2)).reshape(N_HEAD_GROUPS, 1, 3 * HEAD_GROUP)

    def lora_vec(vec):
        t = vec.reshape(1, -1)
        return jnp.concatenate([_pad_cols(t[:, c_dw:c_da], RANK_LORA_PAD), _pad_cols(t[:, c_da:c_dg], RANK_LORA_PAD),
                                t[:, c_dg:c_q]], axis=1)

    row = lambda t: t.reshape(1, -1)
    vecs = dict(
        mu_prev3=regroup(shift_prev[l][:c_dw]), mu_next3=regroup(shift_next[l][:c_dw]),
        lmu_prev=lora_vec(shift_prev[l]), lmu_next=lora_vec(shift_next[l]),
        w0f=row(decay_bias_fwd[l]), w0b=row(decay_bias_bwd[l]), a0f=row(iclr_bias_fwd[l]), a0b=row(iclr_bias_bwd[l]),
        k_k=row(k_k[l]), k_a=row(k_a[l]), r_k=row(r_k[l]))
    mats = dict(
        w2f=_pad_rows(decay_up_fwd[l], RANK_LORA_PAD).astype(BF16), w2b=_pad_rows(decay_up_bwd[l], RANK_LORA_PAD).astype(BF16),
        a2f=_pad_rows(iclr_up_fwd[l], RANK_LORA_PAD).astype(BF16), a2b=_pad_rows(iclr_up_bwd[l], RANK_LORA_PAD).astype(BF16),
        g2=gate_up[l].astype(BF16))

    h = _prenorm(x2, attn_pre_norm[l])
    col_scale = jnp.ones((1, N_QG), F32).at[:, :D_DIFF].set(DIFF_HEAD ** -0.5 * LOG2E)
    rkv3 = _matmul(h, w_rkv, BF16, tm=1024, tn=512, name="in_proj_rkv").reshape(bsz, seq, 3 * D_RWKV)
    lora3 = _matmul(h, w_lora, BF16, tm=1024, tn=LORA_W, name="in_proj_lora").reshape(bsz, seq, LORA_W)
    qg = _matmul(h, w_qg, BF16, tm=1024, tn=512, col_scale=col_scale, name="in_proj_qg")

    r, v, kk, kd_f, kd_b, beta_f, beta_b, lw_f, lw_b, g, bonus = _rwkv_prep(rkv3, lora3, vecs, mats)
    o_f, o_b = _rwkv_scan(r, v, kk, kd_f, kd_b, beta_f, beta_b, lw_f, lw_b)
    y_a = _rwkv_post(o_f, o_b, bonus, g, row(ln_x_gain[l]), row(ln_x_bias[l]))

    y_b = _diff_attn(qg.reshape(bsz, seq, N_QG), lambda_q1[l], lambda_k1[l], lambda_q2[l], lambda_k2[l],
                     subln_gain[l])

    mixed = _upgate(y_a.reshape(m, D_RWKV), y_b.reshape(m, D_DIFF),
                    w_up_rwkv[l].astype(BF16), w_up_diff[l].astype(BF16), qg)
    z = _matmul(mixed, w_out[l].astype(BF16), F32, tm=1024, tn=512, name="out_proj")
    x1, h2 = _resnorm(x2, z, attn_post_norm[l], mlp_pre_norm[l])

    u = _matmul(h2, w_mlp_in[l].astype(BF16), BF16, tm=1024, tn=512, act="relu2", name="mlp_in")
    z2 = _matmul_ktiled(u, w_mlp_out[l].astype(BF16), F32, tm=1024, tn=1024, tk=2048, name="mlp_out")
    out = _resnorm(x1, z2, mlp_post_norm[l])
    return out.reshape(bsz, seq, d)
```

```python
import functools
import math

import jax
import jax.numpy as jnp
from jax import lax
from jax.experimental import pallas as pl
from jax.experimental.pallas import tpu as pltpu

F32 = jnp.float32
BF16 = jnp.bfloat16

D_MODEL = 4096
D_RWKV = D_MODEL // 2
RWKV_HEAD = 64
RANK_LORA = 96
RANK_LORA_PAD = 128
RANK_GATE = 256
D_DIFF = D_MODEL // 2
DIFF_HEAD = 64
N_DIFF_HEADS = D_DIFF // (2 * DIFF_HEAD)
DIFF_VDIM = 2 * DIFF_HEAD
D_FF = 4 * D_MODEL
EPS_RMS = 1e-6
EPS_GN = 64e-5
EPS_SUBLN = 1e-5
LAMBDA_INIT = 0.8 - 0.6 * math.exp(-0.3 * 0)
LOG2E = math.log2(math.e)

HEAD_GROUP = 256
N_HEAD_GROUPS = D_RWKV // HEAD_GROUP
LORA_W = 2 * RANK_LORA_PAD + RANK_GATE
COL_GATE = 3 * D_DIFF
N_QG = COL_GATE + 2 * D_MODEL

CHUNK = 64
SCAN_GROUPS = 4
ATTN_SUM_ROWS = 16
ATTN_KEY_CHUNK = 512
ATTN_ROW_GROUP = 32
VMEM_LIMIT = 48 * 1024 * 1024


def _cparams(sem):
    return pltpu.CompilerParams(dimension_semantics=sem, vmem_limit_bytes=VMEM_LIMIT)


def _rms(t, g):
    return t * lax.rsqrt(jnp.mean(t * t, axis=-1, keepdims=True) + EPS_RMS) * g


def _prenorm_body(x_ref, g_ref, o_ref):
    o_ref[...] = _rms(x_ref[...], g_ref[...]).astype(o_ref.dtype)


def _prenorm(x, g, tr=256):
    m, d = x.shape
    return pl.pallas_call(
        _prenorm_body,
        grid=(m // tr,),
        in_specs=[pl.BlockSpec((tr, d), lambda i: (i, 0)), pl.BlockSpec((1, d), lambda i: (0, 0))],
        out_specs=pl.BlockSpec((tr, d), lambda i: (i, 0)),
        out_shape=jax.ShapeDtypeStruct((m, d), BF16),
        compiler_params=_cparams(("parallel",)),
        name="prenorm",
    )(x, g.reshape(1, d))


def _resnorm_body(x_ref, z_ref, g_ref, g2_ref, o_ref, h_ref):
    y = x_ref[...] + _rms(z_ref[...], g_ref[...])
    o_ref[...] = y
    if h_ref is not None:
        h_ref[...] = _rms(y, g2_ref[...]).astype(h_ref.dtype)


def _resnorm(x, z, g, g_next=None, tr=256):
    m, d = x.shape
    row = pl.BlockSpec((tr, d), lambda i: (i, 0))
    vec = pl.BlockSpec((1, d), lambda i: (0, 0))
    if g_next is None:
        body = lambda x_ref, z_ref, g_ref, o_ref: _resnorm_body(x_ref, z_ref, g_ref, None, o_ref, None)
        return pl.pallas_call(
            body, grid=(m // tr,), in_specs=[row, row, vec], out_specs=row,
            out_shape=jax.ShapeDtypeStruct((m, d), F32),
            compiler_params=_cparams(("parallel",)), name="resnorm_out",
        )(x, z, g.reshape(1, d))
    return pl.pallas_call(
        _resnorm_body, grid=(m // tr,), in_specs=[row, row, vec, vec], out_specs=[row, row],
        out_shape=[jax.ShapeDtypeStruct((m, d), F32), jax.ShapeDtypeStruct((m, d), BF16)],
        compiler_params=_cparams(("parallel",)), name="resnorm_mid",
    )(x, z, g.reshape(1, d), g_next.reshape(1, d))


def _mm_body(a_ref, b_ref, *rest, act, b_transposed):
    o_ref = rest[-1]
    dims = (((1,), (1 if b_transposed else 0,)), ((), ()))
    acc = lax.dot_general(a_ref[...], b_ref[...], dims, preferred_element_type=F32)
    if act == "relu2":
        acc = jnp.square(jnp.maximum(acc, 0.0))
    if len(rest) == 2:
        acc = acc * rest[0][...]
    o_ref[...] = acc.astype(o_ref.dtype)


def _matmul(a, b, out_dtype, tm, tn, act=None, col_scale=None, bt_rows=None, name="matmul"):
    m, k = a.shape
    b_transposed = bt_rows is not None
    if b_transposed:
        first, n = bt_rows
        b_spec = pl.BlockSpec((pl.Element(tn), pl.Element(k)), lambda i, j: (pl.multiple_of(first + j * tn, 16), 0))
    else:
        n = b.shape[1]
        b_spec = pl.BlockSpec((k, tn), lambda i, j: (0, j))
    in_specs = [pl.BlockSpec((tm, k), lambda i, j: (i, 0)), b_spec]
    args = (a, b)
    body = functools.partial(_mm_body, act=act, b_transposed=b_transposed)
    if col_scale is not None:
        in_specs.append(pl.BlockSpec((1, tn), lambda i, j: (0, j)))
        args = (a, b, col_scale)
    return pl.pallas_call(
        body,
        grid=(m // tm, n // tn),
        in_specs=in_specs,
        out_specs=pl.BlockSpec((tm, tn), lambda i, j: (i, j)),
        out_shape=jax.ShapeDtypeStruct((m, n), out_dtype),
        compiler_params=_cparams(("parallel", "arbitrary")),
        name=name,
    )(*args)


def _mmk_body(a_ref, b_ref, o_ref, acc_ref):
    kk = pl.program_id(2)

    @pl.when(kk == 0)
    def _():
        acc_ref[...] = jnp.zeros_like(acc_ref)

    acc_ref[...] += jnp.dot(a_ref[...], b_ref[...], preferred_element_type=F32)

    @pl.when(kk == pl.num_programs(2) - 1)
    def _():
        o_ref[...] = acc_ref[...].astype(o_ref.dtype)


def _matmul_ktiled(a, b, out_dtype, tm, tn, tk, name="matmul_k"):
    m, k = a.shape
    _, n = b.shape
    return pl.pallas_call(
        _mmk_body,
        grid=(m // tm, n // tn, k // tk),
        in_specs=[pl.BlockSpec((tm, tk), lambda i, j, q: (i, q)), pl.BlockSpec((tk, tn), lambda i, j, q: (q, j))],
        out_specs=pl.BlockSpec((tm, tn), lambda i, j, q: (i, j)),
        out_shape=jax.ShapeDtypeStruct((m, n), out_dtype),
        scratch_shapes=[pltpu.VMEM((tm, tn), F32)],
        compiler_params=_cparams(("parallel", "parallel", "arbitrary")),
        name=name,
    )(a, b)


def _sigmoid(t):
    return 1.0 / (1.0 + jnp.exp(-t))


def _upgate_body(ya_ref, yb_ref, wa_ref, wb_ref, ga_ref, gb_ref, o_ref):
    acc_a = jnp.dot(ya_ref[...], wa_ref[...], preferred_element_type=F32)
    acc_b = jnp.dot(yb_ref[...], wb_ref[...], preferred_element_type=F32)
    ga = _sigmoid(ga_ref[...].astype(F32))
    gb = _sigmoid(gb_ref[...].astype(F32))
    o_ref[...] = (ga * acc_a + gb * acc_b).astype(o_ref.dtype)


def _upgate(ya, yb, wa, wb, qg, tm=1024, tn=512):
    m, k = ya.shape
    n = wa.shape[1]
    ga0 = COL_GATE // tn
    gb0 = (COL_GATE + D_MODEL) // tn
    return pl.pallas_call(
        _upgate_body,
        grid=(m // tm, n // tn),
        in_specs=[
            pl.BlockSpec((tm, k), lambda i, j: (i, 0)),
            pl.BlockSpec((tm, k), lambda i, j: (i, 0)),
            pl.BlockSpec((k, tn), lambda i, j: (0, j)),
            pl.BlockSpec((k, tn), lambda i, j: (0, j)),
            pl.BlockSpec((tm, tn), lambda i, j: (i, ga0 + j)),
            pl.BlockSpec((tm, tn), lambda i, j: (i, gb0 + j)),
        ],
        out_specs=pl.BlockSpec((tm, tn), lambda i, j: (i, j)),
        out_shape=jax.ShapeDtypeStruct((m, n), BF16),
        compiler_params=_cparams(("parallel", "arbitrary")),
        name="upgate",
    )(ya, yb, wa, wb, qg, qg)


def _head_mask(rows, cols):
    r = lax.broadcasted_iota(jnp.int32, (rows, cols), 0) // RWKV_HEAD
    c = lax.broadcasted_iota(jnp.int32, (rows, cols), 1) // RWKV_HEAD
    return r == c


def _split_bf16(t):
    hi = t.astype(BF16)
    lo = (t - hi.astype(F32)).astype(BF16)
    return hi, lo


def _head_sum(t, ones_bd):
    hi, lo = _split_bf16(t)
    return (jnp.dot(hi, ones_bd, preferred_element_type=F32)
            + jnp.dot(lo, ones_bd, preferred_element_type=F32))


def _prep_body(r_ref, k_ref, v_ref, lo_ref,
               rp_ref, kp_ref, vp_ref, lop_ref, rn_ref, kn_ref, vn_ref, lon_ref,
               mup_ref, mun_ref, lmup_ref, lmun_ref,
               w0f_ref, w0b_ref, a0f_ref, a0b_ref, kk_ref, ka_ref, rk_ref,
               w2f_ref, w2b_ref, a2f_ref, a2b_ref, g2_ref,
               ro_ref, vo_ref, kko_ref, kdf_ref, kdb_ref, bf_ref, bb_ref, lwf_ref, lwb_ref,
               g_ref, bonus_ref,
               tw_s, da_s, sg_s, *, tb_rows):
    tb = pl.program_id(1)
    hg = pl.program_id(2)
    first = tb == 0
    last = tb == pl.num_programs(1) - 1

    def shift_mix(x_ref, p_ref, n_ref, mup, mun):
        x = x_ref[0].astype(F32)
        width = x.shape[1]
        prev_row = jnp.where(first, 0.0, p_ref[0][7:8, :].astype(F32))
        next_row = jnp.where(last, 0.0, n_ref[0][0:1, :].astype(F32))
        row = lax.broadcasted_iota(jnp.int32, (tb_rows, width), 0)
        xp = jnp.where(row == 0, prev_row, pltpu.roll(x, 1, axis=0))
        xn = jnp.where(row == tb_rows - 1, next_row, pltpu.roll(x, tb_rows - 1, axis=0))
        return x + mup * (xp - x) + mun * (xn - x)

    @pl.when(hg == 0)
    def _():
        lo = shift_mix(lo_ref, lop_ref, lon_ref, lmup_ref[...], lmun_ref[...])
        tw_s[...] = jnp.tanh(lo[:, :RANK_LORA_PAD]).astype(BF16)
        da_s[...] = lo[:, RANK_LORA_PAD:2 * RANK_LORA_PAD].astype(BF16)
        sg_s[...] = _sigmoid(lo[:, 2 * RANK_LORA_PAD:]).astype(BF16)

    mu_p = mup_ref[...]
    mu_n = mun_ref[...]
    r = shift_mix(r_ref, rp_ref, rn_ref, mu_p[:, 0:HEAD_GROUP], mu_n[:, 0:HEAD_GROUP])
    k = shift_mix(k_ref, kp_ref, kn_ref, mu_p[:, HEAD_GROUP:2 * HEAD_GROUP], mu_n[:, HEAD_GROUP:2 * HEAD_GROUP])
    v = shift_mix(v_ref, vp_ref, vn_ref, mu_p[:, 2 * HEAD_GROUP:], mu_n[:, 2 * HEAD_GROUP:])

    ones_bd = jnp.where(_head_mask(HEAD_GROUP, HEAD_GROUP), 1.0, 0.0).astype(BF16)
    kk = k * kk_ref[...]
    kk = kk * lax.rsqrt(jnp.maximum(_head_sum(kk * kk, ones_bd), 1e-24))

    tw = tw_s[...]
    da = da_s[...]

    def direction(w0_ref, w2_ref, a0_ref, a2_ref):
        dec = w0_ref[...] + jnp.dot(tw, w2_ref[...], preferred_element_type=F32)
        z = -dec
        softplus = jnp.maximum(z, 0.0) + jnp.log1p(jnp.exp(-jnp.abs(z)))
        lw = -jnp.exp(-softplus - 0.5)
        a = _sigmoid(a0_ref[...] + jnp.dot(da, a2_ref[...], preferred_element_type=F32))
        kd = k * (1.0 + (a - 1.0) * ka_ref[...])
        return lw, a, kd

    lw_f, a_f, kd_f = direction(w0f_ref, w2f_ref, a0f_ref, a2f_ref)
    lw_b, a_b, kd_b = direction(w0b_ref, w2b_ref, a0b_ref, a2b_ref)

    bonus = _head_sum(r * (0.5 * (kd_f + kd_b)) * rk_ref[...], ones_bd) * v
    g = jnp.dot(sg_s[...], g2_ref[...], preferred_element_type=F32)

    ro_ref[0] = r.astype(ro_ref.dtype)
    vo_ref[0] = v.astype(vo_ref.dtype)
    kko_ref[0] = kk.astype(kko_ref.dtype)
    kdf_ref[0] = kd_f.astype(kdf_ref.dtype)
    kdb_ref[0] = kd_b.astype(kdb_ref.dtype)
    bf_ref[0] = (kk * a_f).astype(bf_ref.dtype)
    bb_ref[0] = (kk * a_b).astype(bb_ref.dtype)
    lwf_ref[0] = lw_f
    lwb_ref[0] = lw_b
    g_ref[0] = g.astype(g_ref.dtype)
    bonus_ref[0] = bonus.astype(bonus_ref.dtype)


def _rwkv_prep(rkv3, lora3, vecs, mats, tb_rows=256):
    bsz, seq, _ = rkv3.shape
    ntb = seq // tb_rows
    hgw = HEAD_GROUP
    n8 = seq // 8
    r0, k0, v0 = 0, D_RWKV // hgw, 2 * D_RWKV // hgw
    lo0 = 0

    def main(c0):
        return pl.BlockSpec((1, tb_rows, hgw), lambda b, t, h: (b, t, c0 + h))

    def prev(c0):
        return pl.BlockSpec((1, 8, hgw), lambda b, t, h: (b, jnp.maximum(t * (tb_rows // 8) - 1, 0), c0 + h))

    def nxt(c0):
        return pl.BlockSpec((1, 8, hgw), lambda b, t, h: (b, jnp.minimum((t + 1) * (tb_rows // 8), n8 - 1), c0 + h))

    lo_main = pl.BlockSpec((1, tb_rows, LORA_W), lambda b, t, h: (b, t, lo0))
    lo_prev = pl.BlockSpec((1, 8, LORA_W), lambda b, t, h: (b, jnp.maximum(t * (tb_rows // 8) - 1, 0), lo0))
    lo_next = pl.BlockSpec((1, 8, LORA_W), lambda b, t, h: (b, jnp.minimum((t + 1) * (tb_rows // 8), n8 - 1), lo0))

    def hvec():
        return pl.BlockSpec((1, hgw), lambda b, t, h: (0, h))

    def hmat(rows):
        return pl.BlockSpec((rows, hgw), lambda b, t, h: (0, h))

    mu3 = pl.BlockSpec((None, 1, 3 * hgw), lambda b, t, h: (h, 0, 0))
    full = lambda shape: pl.BlockSpec(shape, lambda b, t, h: (0,) * len(shape))

    in_specs = [main(r0), main(k0), main(v0), lo_main,
                prev(r0), prev(k0), prev(v0), lo_prev, nxt(r0), nxt(k0), nxt(v0), lo_next,
                mu3, mu3, full((1, LORA_W)), full((1, LORA_W)),
                hvec(), hvec(), hvec(), hvec(), hvec(), hvec(), hvec(),
                hmat(RANK_LORA_PAD), hmat(RANK_LORA_PAD), hmat(RANK_LORA_PAD), hmat(RANK_LORA_PAD),
                hmat(RANK_GATE)]
    out_block = pl.BlockSpec((1, tb_rows, hgw), lambda b, t, h: (b, t, h))
    shp = lambda dt: jax.ShapeDtypeStruct((bsz, seq, D_RWKV), dt)
    out_dtypes = [BF16, BF16, BF16, BF16, BF16, BF16, BF16, F32, F32, BF16, F32]
    return pl.pallas_call(
        functools.partial(_prep_body, tb_rows=tb_rows),
        grid=(bsz, ntb, N_HEAD_GROUPS),
        in_specs=in_specs,
        out_specs=[out_block] * len(out_dtypes),
        out_shape=[shp(dt) for dt in out_dtypes],
        scratch_shapes=[pltpu.VMEM((tb_rows, RANK_LORA_PAD), BF16),
                        pltpu.VMEM((tb_rows, RANK_LORA_PAD), BF16),
                        pltpu.VMEM((tb_rows, RANK_GATE), BF16)],
        compiler_params=_cparams(("parallel", "parallel", "arbitrary")),
        name="rwkv_prep",
    )(rkv3, rkv3, rkv3, lora3, rkv3, rkv3, rkv3, lora3, rkv3, rkv3, rkv3, lora3,
      vecs["mu_prev3"], vecs["mu_next3"], vecs["lmu_prev"], vecs["lmu_next"],
      vecs["w0f"], vecs["w0b"], vecs["a0f"], vecs["a0b"], vecs["k_k"], vecs["k_a"], vecs["r_k"],
      mats["w2f"], mats["w2b"], mats["a2f"], mats["a2b"], mats["g2"])


def _dot_nt(a, b):
    return lax.dot_general(a, b, (((1,), (1,)), ((), ())), preferred_element_type=F32)


def _dot_tn(a, b):
    return lax.dot_general(a, b, (((0,), (0,)), ((), ())), preferred_element_type=F32)


def _block_diag(t, mask):
    return jnp.where(mask, jnp.concatenate([t, t, t, t], axis=0), jnp.zeros((), t.dtype))


def _diag_blocks(full, colhead):
    out = full[0:RWKV_HEAD]
    for h in range(1, HEAD_GROUP // RWKV_HEAD):
        out = jnp.where(colhead == h, full[h * RWKV_HEAD:(h + 1) * RWKV_HEAD], out)
    return out


def _scan_masks():
    c = CHUNK
    t_idx = lax.broadcasted_iota(jnp.int32, (c, HEAD_GROUP), 0)
    s_idx = lax.broadcasted_iota(jnp.int32, (c, HEAD_GROUP), 1) % c
    n_idx = lax.broadcasted_iota(jnp.int32, (RWKV_HEAD, HEAD_GROUP), 0)
    j_idx = lax.broadcasted_iota(jnp.int32, (RWKV_HEAD, HEAD_GROUP), 1)
    return dict(
        bd=_head_mask(HEAD_GROUP, HEAD_GROUP),
        before={False: s_idx < t_idx, True: s_idx > t_idx},
        upto={False: s_idx <= t_idx, True: s_idx >= t_idx},
        eye=jnp.where(s_idx == t_idx, 1.0, 0.0),
        colhead=j_idx // RWKV_HEAD,
        diag=n_idx == j_idx % RWKV_HEAD)


def _scan_chunks(chains, mk):
    c = CHUNK
    bd_mask = mk["bd"]
    n = len(chains)
    rev = [ch["reverse"] for ch in chains]
    strict = [mk["before"][x] for x in rev]
    incl = [mk["upto"][x] for x in rev]
    bd = lambda t: _block_diag(t, bd_mask)
    mm = lambda a, b: jnp.dot(a, b, preferred_element_type=F32)
    each = lambda f, *cols: [f(*xs) for xs in zip(*cols)]

    tri = {x: jnp.where(mk["upto"][x][:, :c], 1.0, 0.0).astype(BF16) for x in set(rev)}
    lw = [ch["lw"] for ch in chains]
    split = each(_split_bf16, lw)
    cum2 = [mm(tri[x], jnp.concatenate([hi, lo], axis=1)) for x, (hi, lo) in zip(rev, split)]
    cum = [t[:, :HEAD_GROUP] + t[:, HEAD_GROUP:] for t in cum2]
    cum_edge = [t[0:1, :] if x else t[c - 1:c, :] for x, t in zip(rev, cum)]
    e_pos = each(jnp.exp, cum)
    e_neg = [jnp.exp(-t) for t in cum]
    g_edge = each(jnp.exp, cum_edge)
    e_edge = each(lambda g, e: g * e, g_edge, e_neg)
    a_t16 = [(-ch["kk"] * jnp.exp(t - l)).astype(BF16) for ch, t, l in zip(chains, cum, lw)]
    r_t = [ch["r"] * e for ch, e in zip(chains, e_pos)]
    b_t16 = [(ch["beta"] * e).astype(BF16) for ch, e in zip(chains, e_neg)]
    k_t16 = [(ch["kd"] * e).astype(BF16) for ch, e in zip(chains, e_neg)]
    b_p16 = [(ch["beta"] * e).astype(BF16) for ch, e in zip(chains, e_edge)]
    k_p16 = [(ch["kd"] * e).astype(BF16) for ch, e in zip(chains, e_edge)]
    v16 = [ch["v"].astype(BF16) for ch in chains]
    bd_v = each(bd, v16)

    ar = [jnp.concatenate([a, r.astype(BF16)], axis=0) for a, r in zip(a_t16, r_t)]
    pb = [_dot_nt(x, bd(b)) for x, b in zip(ar, b_t16)]
    pk = [_dot_nt(x, bd(k)) for x, k in zip(ar, k_t16)]
    p = [jnp.where(m, t[:c], 0.0) for m, t in zip(strict, pb)]
    q16 = [jnp.where(m, t[:c], 0.0).astype(BF16) for m, t in zip(strict, pk)]
    mrb16 = [jnp.where(m, t[c:], 0.0).astype(BF16) for m, t in zip(incl, pb)]
    mrk16 = [jnp.where(m, t[c:], 0.0).astype(BF16) for m, t in zip(incl, pk)]

    qv16 = [mm(a, b).astype(BF16) for a, b in zip(q16, bd_v)]

    tmat = [mk["eye"] + t for t in p]
    p16 = [t.astype(BF16) for t in p]
    pw = [mm(t, bd(t)) for t in p16]
    levels = int(math.log2(c))
    for lev in range(1, levels):
        bd_pw = [bd(t.astype(BF16)) for t in pw]
        if lev < levels - 1:
            both = [mm(jnp.concatenate([t, w], axis=0).astype(BF16), b) for t, w, b in zip(tmat, pw, bd_pw)]
            tmat = [t + x[:c] for t, x in zip(tmat, both)]
            pw = [x[c:] for x in both]
        else:
            tmat = [t + mm(t.astype(BF16), b) for t, b in zip(tmat, bd_pw)]

    t16 = [t.astype(BF16) for t in tmat]
    a16 = [mm(t, bd(a)).astype(BF16) for t, a in zip(t16, a_t16)]
    w16 = [mm(t, bd(x)).astype(BF16) for t, x in zip(t16, qv16)]

    g_full = [_dot_tn(b, a) for b, a in zip(b_p16, a16)]
    h_full = [_dot_tn(jnp.concatenate([b, k], axis=0), jnp.concatenate([w, v], axis=0))
              for b, k, w, v in zip(b_p16, k_p16, w16, v16)]
    g_mat = [_diag_blocks(t, mk["colhead"]) + jnp.where(mk["diag"], g, 0.0) for t, g in zip(g_full, g_edge)]
    h_mat = [_diag_blocks(t, mk["colhead"]) for t in h_full]

    r_hat = [r + mm(m, bd(a)) for r, m, a in zip(r_t, mrb16, a16)]
    o_intra = [mm(m, bd(w)) + mm(mk_, bv) for m, w, mk_, bv in zip(mrb16, w16, mrk16, bd_v)]

    both = [mm(jnp.concatenate([r, g], axis=0).astype(BF16), bd(ch["state"].astype(BF16)))
            for r, g, ch in zip(r_hat, g_mat, chains)]
    outs = [x[:c] + o for x, o in zip(both, o_intra)]
    states = [x[c:] + h for x, h in zip(both, h_mat)]
    return outs, states


def _scan_body(rf_ref, vf_ref, kkf_ref, kdf_ref, bf_ref, lwf_ref,
               rb_ref, vb_ref, kkb_ref, kdb_ref, bb_ref, lwb_ref,
               of_ref, ob_ref, sf_ref, sb_ref, *, groups):
    @pl.when(pl.program_id(2) == 0)
    def _():
        sf_ref[...] = jnp.zeros_like(sf_ref)
        sb_ref[...] = jnp.zeros_like(sb_ref)

    mk = _scan_masks()
    chains, sinks = [], []
    for gi in range(groups):
        cols = slice(gi * HEAD_GROUP, (gi + 1) * HEAD_GROUP)
        ld = lambda ref: ref[0, :, cols].astype(F32)
        chains.append(dict(r=ld(rf_ref), v=ld(vf_ref), kk=ld(kkf_ref), kd=ld(kdf_ref), beta=ld(bf_ref),
                           lw=ld(lwf_ref), state=sf_ref[:, cols], reverse=False))
        sinks.append((of_ref, sf_ref, cols))
        chains.append(dict(r=ld(rb_ref), v=ld(vb_ref), kk=ld(kkb_ref), kd=ld(kdb_ref), beta=ld(bb_ref),
                           lw=ld(lwb_ref), state=sb_ref[:, cols], reverse=True))
        sinks.append((ob_ref, sb_ref, cols))
    outs, states = _scan_chunks(chains, mk)
    for (o_ref, s_ref, cols), o, s in zip(sinks, outs, states):
        o_ref[0, :, cols] = o
        s_ref[:, cols] = s


def _rwkv_scan(r, v, kk, kd_f, kd_b, beta_f, beta_b, lw_f, lw_b, groups=SCAN_GROUPS):
    bsz, seq, _ = r.shape
    nc = seq // CHUNK
    width = groups * HEAD_GROUP
    fwd = pl.BlockSpec((1, CHUNK, width), lambda b, h, c: (b, c, h))
    bwd = pl.BlockSpec((1, CHUNK, width), lambda b, h, c: (b, nc - 1 - c, h))
    out = jax.ShapeDtypeStruct((bsz, seq, D_RWKV), F32)
    return pl.pallas_call(
        functools.partial(_scan_body, groups=groups),
        grid=(bsz, D_RWKV // width, nc),
        in_specs=[fwd] * 6 + [bwd] * 6,
        out_specs=[fwd, bwd],
        out_shape=[out, out],
        scratch_shapes=[pltpu.VMEM((RWKV_HEAD, width), F32), pltpu.VMEM((RWKV_HEAD, width), F32)],
        compiler_params=_cparams(("parallel", "parallel", "arbitrary")),
        name="rwkv_scan",
    )(r, v, kk, kd_f, beta_f, lw_f, r, v, kk, kd_b, beta_b, lw_b)


def _post_body(of_ref, ob_ref, bonus_ref, g_ref, lng_ref, lnb_ref, y_ref):
    ones_bd = jnp.where(_head_mask(HEAD_GROUP, HEAD_GROUP), 1.0, 0.0).astype(BF16)
    o = of_ref[0] + ob_ref[0]
    inv_n = 1.0 / RWKV_HEAD
    mu = _head_sum(o, ones_bd) * inv_n
    d = o - mu
    var = _head_sum(d * d, ones_bd) * inv_n
    on = d * lax.rsqrt(var + EPS_GN) * lng_ref[...] + lnb_ref[...]
    y_ref[0] = ((on + bonus_ref[0]) * g_ref[0].astype(F32)).astype(y_ref.dtype)


def _rwkv_post(o_f, o_b, bonus, g, ln_g, ln_b, tb_rows=512):
    bsz, seq, _ = o_f.shape
    blk = pl.BlockSpec((1, tb_rows, HEAD_GROUP), lambda b, t, h: (b, t, h))
    vec = pl.BlockSpec((1, HEAD_GROUP), lambda b, t, h: (0, h))
    return pl.pallas_call(
        _post_body,
        grid=(bsz, seq // tb_rows, N_HEAD_GROUPS),
        in_specs=[blk, blk, blk, blk, vec, vec],
        out_specs=blk,
        out_shape=jax.ShapeDtypeStruct((bsz, seq, D_RWKV), BF16),
        compiler_params=_cparams(("parallel", "parallel", "parallel")),
        name="rwkv_post",
    )(o_f, o_b, bonus, g, ln_g, ln_b)


def _attn_body(qc_ref, qn_ref, k_ref, v_ref, lq1_ref, lk1_ref, lq2_ref, lk2_ref, sg_ref, o_ref,
               tbl_ref, vt_ref, t_ref, e_ref, *, tq):
    head = pl.program_id(1)
    qb = pl.program_id(2)
    nq = pl.num_programs(2)
    seq = k_ref.shape[1]
    d = DIFF_HEAD
    n_chunk = seq // ATTN_KEY_CHUNK
    n_quarter = 4
    rows_q = seq // n_quarter

    def stacked(q):
        lane = lax.broadcasted_iota(jnp.int32, (tq, 2 * d), 1)
        zero = jnp.zeros((), BF16)
        return jnp.concatenate([jnp.where(lane < d, q, zero), jnp.where(lane >= d, q, zero)], axis=0)

    def score_chunk(slot, qq, blk, c):
        rows = slice(c * ATTN_KEY_CHUNK, (c + 1) * ATTN_KEY_CHUNK)
        start = seq - tq - blk * tq + c * ATTN_KEY_CHUNK
        bias = tbl_ref[pl.ds(pl.multiple_of(start, 8), ATTN_KEY_CHUNK), :]
        t_ref[slot, rows, :] = _dot_nt(k_ref[0, rows, :], qq) + jnp.concatenate([bias, bias], axis=1)

    @pl.when(qb == 0)
    def _():
        slope = jnp.exp2(jnp.full((1, 1), -8.0 / N_DIFF_HEADS, F32) * (head + 1).astype(F32)) * LOG2E
        row = lax.broadcasted_iota(jnp.int32, tbl_ref.shape, 0)
        col = lax.broadcasted_iota(jnp.int32, tbl_ref.shape, 1)
        tbl_ref[...] = -slope * jnp.abs(col - row + (seq - tq)).astype(F32)
        vt_ref[0:2 * d, :] = v_ref[0].astype(F32).T.astype(BF16)
        ones_row = lax.broadcasted_iota(jnp.int32, (ATTN_SUM_ROWS, seq), 0) == 0
        vt_ref[2 * d:, :] = jnp.where(ones_row, 1.0, 0.0).astype(BF16)
        qq0 = stacked(qc_ref[0])
        for c in range(n_chunk):
            score_chunk(0, qq0, 0, c)

    lam = (jnp.exp(jnp.sum(lq1_ref[...] * lk1_ref[...], axis=-1, keepdims=True))
           - jnp.exp(jnp.sum(lq2_ref[...] * lk2_ref[...], axis=-1, keepdims=True)) + LAMBDA_INIT)

    blk_next = jnp.minimum(qb + 1, nq - 1)

    def step(cur, nxt):
        qq_next = stacked(qn_ref[0])
        next_chunks = iter(range(n_chunk))

        def issue_scores(count):
            for _ in range(count):
                c = next(next_chunks, None)
                if c is not None:
                    score_chunk(nxt, qq_next, blk_next, c)

        m_part = None
        for qtr in range(n_quarter):
            issue_scores(1)
            for g in range(rows_q // ATTN_ROW_GROUP):
                r0 = qtr * rows_q + g * ATTN_ROW_GROUP
                x = t_ref[cur, r0:r0 + ATTN_ROW_GROUP, :]
                for r in range(ATTN_ROW_GROUP // 8):
                    tile = x[r * 8:(r + 1) * 8, :]
                    m_part = tile if m_part is None else jnp.maximum(m_part, tile)
        m = jnp.max(m_part, axis=0, keepdims=True)

        aug = None
        for qtr in range(n_quarter):
            issue_scores(1)
            for g in range(rows_q // ATTN_ROW_GROUP):
                r0 = qtr * rows_q + g * ATTN_ROW_GROUP
                e_ref[r0:r0 + ATTN_ROW_GROUP, :] = jnp.exp2(
                    t_ref[cur, r0:r0 + ATTN_ROW_GROUP, :] - m).astype(BF16)
            rows = slice(qtr * rows_q, (qtr + 1) * rows_q)
            part = jnp.dot(vt_ref[:, rows], e_ref[rows, :], preferred_element_type=F32)
            aug = part if aug is None else aug + part
        issue_scores(n_chunk)

        acc = aug[:2 * d]
        l = aug[2 * d:2 * d + 1]
        out_t = acc[:, :tq] * (1.0 / l[:, :tq]) - acc[:, tq:] * (lam / l[:, tq:])
        out = out_t.T
        out = out * lax.rsqrt(jnp.mean(out * out, axis=-1, keepdims=True) + EPS_SUBLN) * sg_ref[...]
        o_ref[0] = (out * (1.0 - LAMBDA_INIT)).astype(o_ref.dtype)

    @pl.when(qb % 2 == 0)
    def _():
        step(0, 1)

    @pl.when(qb % 2 == 1)
    def _():
        step(1, 0)


def _diff_attn(qg3, lq1, lk1, lq2, lk2, subln_g, tq=256):
    bsz, seq, _ = qg3.shape
    w = DIFF_VDIM
    nq = seq // tq
    q0 = 0
    k0 = D_DIFF // w
    v0 = 2 * D_DIFF // w
    vec = lambda n: pl.BlockSpec((1, n), lambda b, h, i: (0, 0))
    return pl.pallas_call(
        functools.partial(_attn_body, tq=tq),
        grid=(bsz, N_DIFF_HEADS, nq),
        in_specs=[pl.BlockSpec((1, tq, w), lambda b, h, i: (b, i, q0 + h)),
                  pl.BlockSpec((1, tq, w), lambda b, h, i: (b, jnp.minimum(i + 1, nq - 1), q0 + h)),
                  pl.BlockSpec((1, seq, w), lambda b, h, i: (b, 0, k0 + h)),
                  pl.BlockSpec((1, seq, w), lambda b, h, i: (b, 0, v0 + h)),
                  vec(DIFF_HEAD), vec(DIFF_HEAD), vec(DIFF_HEAD), vec(DIFF_HEAD), vec(w)],
        out_specs=pl.BlockSpec((1, tq, w), lambda b, h, i: (b, i, h)),
        out_shape=jax.ShapeDtypeStruct((bsz, seq, D_DIFF), BF16),
        scratch_shapes=[pltpu.VMEM((2 * seq - tq, tq), F32),
                        pltpu.VMEM((w + ATTN_SUM_ROWS, seq), BF16),
                        pltpu.VMEM((2, seq, 2 * tq), F32),
                        pltpu.VMEM((seq, 2 * tq), BF16)],
        compiler_params=_cparams(("parallel", "parallel", "arbitrary")),
        name="diff_attn",
    )(qg3, qg3, qg3, qg3, lq1.reshape(1, -1), lk1.reshape(1, -1), lq2.reshape(1, -1), lk2.reshape(1, -1),
      subln_g.reshape(1, -1))


def _pad_cols(t, width):
    return jnp.pad(t, ((0, 0), (0, width - t.shape[1])))


def _pad_rows(t, rows):
    return jnp.pad(t, ((0, rows - t.shape[0]), (0, 0)))


def kernel(x, attn_pre_norm, attn_post_norm, w_in, shift_prev, shift_next, decay_bias_fwd, decay_up_fwd, decay_bias_bwd, decay_up_bwd, iclr_bias_fwd, iclr_up_fwd, iclr_bias_bwd, iclr_up_bwd, gate_up, k_k, k_a, r_k, ln_x_gain, ln_x_bias, lambda_q1, lambda_k1, lambda_q2, lambda_k2, subln_gain, w_up_rwkv, w_up_diff, w_out, mlp_pre_norm, mlp_post_norm, w_mlp_in, w_mlp_out):
    bsz, seq, d = x.shape
    m = bsz * seq
    l = 0
    x2 = x.reshape(m, d)

    wt = jnp.transpose(w_in[l]).astype(BF16)
    c_dw = 3 * D_RWKV
    c_da = c_dw + RANK_LORA
    c_dg = c_da + RANK_LORA
    c_q = c_dg + RANK_GATE
    wt_lora = jnp.concatenate([_pad_rows(wt[c_dw:c_da], RANK_LORA_PAD), _pad_rows(wt[c_da:c_dg], RANK_LORA_PAD),
                               wt[c_dg:c_q]], axis=0)

    def regroup(vec):
        t = vec.reshape(3, N_HEAD_GROUPS, HEAD_GROUP)
        return jnp.transpose(t, (1, 0, 2)).reshape(N_HEAD_GROUPS, 1, 3 * HEAD_GROUP)

    def lora_vec(vec):
        t = vec.reshape(1, -1)
        return jnp.concatenate([_pad_cols(t[:, c_dw:c_da], RANK_LORA_PAD), _pad_cols(t[:, c_da:c_dg], RANK_LORA_PAD),
                                t[:, c_dg:c_q]], axis=1)

    row = lambda t: t.reshape(1, -1)
    vecs = dict(
        mu_prev3=regroup(shift_prev[l][:c_dw]), mu_next3=regroup(shift_next[l][:c_dw]),
        lmu_prev=lora_vec(shift_prev[l]), lmu_next=lora_vec(shift_next[l]),
        w0f=row(decay_bias_fwd[l]), w0b=row(decay_bias_bwd[l]), a0f=row(iclr_bias_fwd[l]), a0b=row(iclr_bias_bwd[l]),
        k_k=row(k_k[l]), k_a=row(k_a[l]), r_k=row(r_k[l]))
    mats = dict(
        w2f=_pad_rows(decay_up_fwd[l], RANK_LORA_PAD).astype(BF16), w2b=_pad_rows(decay_up_bwd[l], RANK_LORA_PAD).astype(BF16),
        a2f=_pad_rows(iclr_up_fwd[l], RANK_LORA_PAD).astype(BF16), a2b=_pad_rows(iclr_up_bwd[l], RANK_LORA_PAD).astype(BF16),
        g2=gate_up[l].astype(BF16))

    h = _prenorm(x2, attn_pre_norm[l])
    col_scale = jnp.ones((1, N_QG), F32).at[:, :D_DIFF].set(DIFF_HEAD ** -0.5 * LOG2E)
    rkv3 = _matmul(h, wt, BF16, tm=1024, tn=512, bt_rows=(0, c_dw),
                   name="in_proj_rkv").reshape(bsz, seq, 3 * D_RWKV)
    lora3 = _matmul(h, wt_lora, BF16, tm=1024, tn=LORA_W, bt_rows=(0, LORA_W),
                    name="in_proj_lora").reshape(bsz, seq, LORA_W)
    qg = _matmul(h, wt, BF16, tm=1024, tn=512, col_scale=col_scale, bt_rows=(c_q, N_QG), name="in_proj_qg")

    r, v, kk, kd_f, kd_b, beta_f, beta_b, lw_f, lw_b, g, bonus = _rwkv_prep(rkv3, lora3, vecs, mats)
    o_f, o_b = _rwkv_scan(r, v, kk, kd_f, kd_b, beta_f, beta_b, lw_f, lw_b)
    y_a = _rwkv_post(o_f, o_b, bonus, g, row(ln_x_gain[l]), row(ln_x_bias[l]))

    y_b = _diff_attn(qg.reshape(bsz, seq, N_QG), lambda_q1[l], lambda_k1[l], lambda_q2[l], lambda_k2[l],
                     subln_gain[l])

    mixed = _upgate(y_a.reshape(m, D_RWKV), y_b.reshape(m, D_DIFF),
                    w_up_rwkv[l].astype(BF16), w_up_diff[l].astype(BF16), qg)
    z = _matmul(mixed, w_out[l].astype(BF16), F32, tm=1024, tn=512, name="out_proj")
    x1, h2 = _resnorm(x2, z, attn_post_norm[l], mlp_pre_norm[l])

    u = _matmul(h2, w_mlp_in[l].astype(BF16), BF16, tm=1024, tn=512, act="relu2", name="mlp_in")
    z2 = _matmul_ktiled(u, w_mlp_out[l].astype(BF16), F32, tm=1024, tn=1024, tk=2048, name="mlp_out")
    out = _resnorm(x1, z2, mlp_post_norm[l])
    return out.reshape(bsz, seq, d)
```

```python
import functools
import math

import jax
import jax.numpy as jnp
from jax import lax
from jax.experimental import pallas as pl
from jax.experimental.pallas import tpu as pltpu

F32 = jnp.float32
BF16 = jnp.bfloat16

D_MODEL = 4096
D_RWKV = D_MODEL // 2
RWKV_HEAD = 64
RANK_LORA = 96
RANK_LORA_PAD = 128
RANK_GATE = 256
D_DIFF = D_MODEL // 2
DIFF_HEAD = 64
N_DIFF_HEADS = D_DIFF // (2 * DIFF_HEAD)
DIFF_VDIM = 2 * DIFF_HEAD
D_FF = 4 * D_MODEL
EPS_RMS = 1e-6
EPS_GN = 64e-5
EPS_SUBLN = 1e-5
LAMBDA_INIT = 0.8 - 0.6 * math.exp(-0.3 * 0)
LOG2E = math.log2(math.e)

HEAD_GROUP = 256
N_HEAD_GROUPS = D_RWKV // HEAD_GROUP
LORA_W = 2 * RANK_LORA_PAD + RANK_GATE
COL_GATE = 3 * D_DIFF
N_QG = COL_GATE + 2 * D_MODEL

CHUNK = 64
SCAN_GROUPS = 4
ATTN_SUM_ROWS = 16
ATTN_KEY_CHUNK = 512
ATTN_ROW_GROUP = 32
VMEM_LIMIT = 48 * 1024 * 1024


def _cparams(sem):
    return pltpu.CompilerParams(dimension_semantics=sem, vmem_limit_bytes=VMEM_LIMIT)


def _rms(t, g):
    return t * lax.rsqrt(jnp.mean(t * t, axis=-1, keepdims=True) + EPS_RMS) * g


def _prenorm_body(x_ref, g_ref, o_ref):
    o_ref[...] = _rms(x_ref[...], g_ref[...]).astype(o_ref.dtype)


def _prenorm(x, g, tr=256):
    m, d = x.shape
    return pl.pallas_call(
        _prenorm_body,
        grid=(m // tr,),
        in_specs=[pl.BlockSpec((tr, d), lambda i: (i, 0)), pl.BlockSpec((1, d), lambda i: (0, 0))],
        out_specs=pl.BlockSpec((tr, d), lambda i: (i, 0)),
        out_shape=jax.ShapeDtypeStruct((m, d), BF16),
        compiler_params=_cparams(("parallel",)),
        name="prenorm",
    )(x, g.reshape(1, d))


def _resnorm_body(x_ref, z_ref, g_ref, g2_ref, o_ref, h_ref):
    y = x_ref[...] + _rms(z_ref[...], g_ref[...])
    o_ref[...] = y
    if h_ref is not None:
        h_ref[...] = _rms(y, g2_ref[...]).astype(h_ref.dtype)


def _resnorm(x, z, g, g_next=None, tr=256):
    m, d = x.shape
    row = pl.BlockSpec((tr, d), lambda i: (i, 0))
    vec = pl.BlockSpec((1, d), lambda i: (0, 0))
    if g_next is None:
        body = lambda x_ref, z_ref, g_ref, o_ref: _resnorm_body(x_ref, z_ref, g_ref, None, o_ref, None)
        return pl.pallas_call(
            body, grid=(m // tr,), in_specs=[row, row, vec], out_specs=row,
            out_shape=jax.ShapeDtypeStruct((m, d), F32),
            compiler_params=_cparams(("parallel",)), name="resnorm_out",
        )(x, z, g.reshape(1, d))
    return pl.pallas_call(
        _resnorm_body, grid=(m // tr,), in_specs=[row, row, vec, vec], out_specs=[row, row],
        out_shape=[jax.ShapeDtypeStruct((m, d), F32), jax.ShapeDtypeStruct((m, d), BF16)],
        compiler_params=_cparams(("parallel",)), name="resnorm_mid",
    )(x, z, g.reshape(1, d), g_next.reshape(1, d))


def _mm_body(a_ref, b_ref, *rest, act, b_transposed, scaled, cast_b):
    if cast_b:
        w16_ref = rest[-1]
        rest = rest[:-1]

        @pl.when(pl.program_id(1) == 0)
        def _():
            w16_ref[...] = b_ref[...].astype(BF16)

        b = w16_ref[...]
    else:
        b = b_ref[...]
    o_ref = rest[-1]
    dims = (((1,), (1 if b_transposed else 0,)), ((), ()))
    acc = lax.dot_general(a_ref[...], b, dims, preferred_element_type=F32)
    if act == "relu2":
        acc = jnp.square(jnp.maximum(acc, 0.0))
    if scaled:
        acc = acc * rest[0][...]
    o_ref[...] = acc.astype(o_ref.dtype)


def _matmul(a, b, out_dtype, tm, tn, act=None, col_scale=None, bt_rows=None, name="matmul"):
    m, k = a.shape
    b_transposed = bt_rows is not None
    cast_b = b.dtype == F32
    ij = (lambda g0, g1: (g1, g0)) if cast_b else (lambda g0, g1: (g0, g1))
    if b_transposed:
        first, n = bt_rows
        b_block = (tn, k)
        b_spec = pl.BlockSpec((pl.Element(tn), pl.Element(k)),
                              lambda g0, g1: (pl.multiple_of(first + ij(g0, g1)[1] * tn, 16), 0))
    else:
        n = b.shape[1]
        b_block = (k, tn)
        b_spec = pl.BlockSpec(b_block, lambda g0, g1: (0, ij(g0, g1)[1]))
    in_specs = [pl.BlockSpec((tm, k), lambda g0, g1: (ij(g0, g1)[0], 0)), b_spec]
    args = (a, b)
    if col_scale is not None:
        in_specs.append(pl.BlockSpec((1, tn), lambda g0, g1: (0, ij(g0, g1)[1])))
        args = (a, b, col_scale)
    body = functools.partial(_mm_body, act=act, b_transposed=b_transposed, scaled=col_scale is not None,
                             cast_b=cast_b)
    return pl.pallas_call(
        body,
        grid=(n // tn, m // tm) if cast_b else (m // tm, n // tn),
        in_specs=in_specs,
        out_specs=pl.BlockSpec((tm, tn), lambda g0, g1: ij(g0, g1)),
        out_shape=jax.ShapeDtypeStruct((m, n), out_dtype),
        scratch_shapes=[pltpu.VMEM(b_block, BF16)] if cast_b else [],
        compiler_params=_cparams(("parallel", "arbitrary")),
        name=name,
    )(*args)


def _mmk_body(a_ref, b_ref, o_ref, acc_ref):
    kk = pl.program_id(2)

    @pl.when(kk == 0)
    def _():
        acc_ref[...] = jnp.zeros_like(acc_ref)

    acc_ref[...] += jnp.dot(a_ref[...], b_ref[...], preferred_element_type=F32)

    @pl.when(kk == pl.num_programs(2) - 1)
    def _():
        o_ref[...] = acc_ref[...].astype(o_ref.dtype)


def _matmul_ktiled(a, b, out_dtype, tm, tn, tk, name="matmul_k"):
    m, k = a.shape
    _, n = b.shape
    return pl.pallas_call(
        _mmk_body,
        grid=(m // tm, n // tn, k // tk),
        in_specs=[pl.BlockSpec((tm, tk), lambda i, j, q: (i, q)), pl.BlockSpec((tk, tn), lambda i, j, q: (q, j))],
        out_specs=pl.BlockSpec((tm, tn), lambda i, j, q: (i, j)),
        out_shape=jax.ShapeDtypeStruct((m, n), out_dtype),
        scratch_shapes=[pltpu.VMEM((tm, tn), F32)],
        compiler_params=_cparams(("parallel", "parallel", "arbitrary")),
        name=name,
    )(a, b)


def _sigmoid(t):
    return 1.0 / (1.0 + jnp.exp(-t))


def _upgate_body(ya_ref, yb_ref, wa_ref, wb_ref, ga_ref, gb_ref, o_ref):
    acc_a = jnp.dot(ya_ref[...], wa_ref[...], preferred_element_type=F32)
    acc_b = jnp.dot(yb_ref[...], wb_ref[...], preferred_element_type=F32)
    ga = _sigmoid(ga_ref[...].astype(F32))
    gb = _sigmoid(gb_ref[...].astype(F32))
    o_ref[...] = (ga * acc_a + gb * acc_b).astype(o_ref.dtype)


def _upgate(ya, yb, wa, wb, qg, tm=1024, tn=512):
    m, k = ya.shape
    n = wa.shape[1]
    ga0 = COL_GATE // tn
    gb0 = (COL_GATE + D_MODEL) // tn
    return pl.pallas_call(
        _upgate_body,
        grid=(m // tm, n // tn),
        in_specs=[
            pl.BlockSpec((tm, k), lambda i, j: (i, 0)),
            pl.BlockSpec((tm, k), lambda i, j: (i, 0)),
            pl.BlockSpec((k, tn), lambda i, j: (0, j)),
            pl.BlockSpec((k, tn), lambda i, j: (0, j)),
            pl.BlockSpec((tm, tn), lambda i, j: (i, ga0 + j)),
            pl.BlockSpec((tm, tn), lambda i, j: (i, gb0 + j)),
        ],
        out_specs=pl.BlockSpec((tm, tn), lambda i, j: (i, j)),
        out_shape=jax.ShapeDtypeStruct((m, n), BF16),
        compiler_params=_cparams(("parallel", "arbitrary")),
        name="upgate",
    )(ya, yb, wa, wb, qg, qg)


def _head_mask(rows, cols):
    r = lax.broadcasted_iota(jnp.int32, (rows, cols), 0) // RWKV_HEAD
    c = lax.broadcasted_iota(jnp.int32, (rows, cols), 1) // RWKV_HEAD
    return r == c


def _split_bf16(t):
    hi = t.astype(BF16)
    lo = (t - hi.astype(F32)).astype(BF16)
    return hi, lo


def _head_sum(t, ones_bd):
    hi, lo = _split_bf16(t)
    return (jnp.dot(hi, ones_bd, preferred_element_type=F32)
            + jnp.dot(lo, ones_bd, preferred_element_type=F32))


def _prep_body(r_ref, k_ref, v_ref, lo_ref,
               rp_ref, kp_ref, vp_ref, lop_ref, rn_ref, kn_ref, vn_ref, lon_ref,
               mup_ref, mun_ref, lmup_ref, lmun_ref,
               w0f_ref, w0b_ref, a0f_ref, a0b_ref, kk_ref, ka_ref, rk_ref,
               w2f_ref, w2b_ref, a2f_ref, a2b_ref, g2_ref,
               ro_ref, vo_ref, kko_ref, kdf_ref, kdb_ref, bf_ref, bb_ref, lwf_ref, lwb_ref,
               g_ref, bonus_ref,
               tw_s, da_s, sg_s, *, tb_rows):
    tb = pl.program_id(1)
    hg = pl.program_id(2)
    first = tb == 0
    last = tb == pl.num_programs(1) - 1

    def shift_mix(x_ref, p_ref, n_ref, mup, mun):
        x = x_ref[0].astype(F32)
        width = x.shape[1]
        prev_row = jnp.where(first, 0.0, p_ref[0][7:8, :].astype(F32))
        next_row = jnp.where(last, 0.0, n_ref[0][0:1, :].astype(F32))
        row = lax.broadcasted_iota(jnp.int32, (tb_rows, width), 0)
        xp = jnp.where(row == 0, prev_row, pltpu.roll(x, 1, axis=0))
        xn = jnp.where(row == tb_rows - 1, next_row, pltpu.roll(x, tb_rows - 1, axis=0))
        return x + mup * (xp - x) + mun * (xn - x)

    @pl.when(hg == 0)
    def _():
        lo = shift_mix(lo_ref, lop_ref, lon_ref, lmup_ref[...], lmun_ref[...])
        tw_s[...] = jnp.tanh(lo[:, :RANK_LORA_PAD]).astype(BF16)
        da_s[...] = lo[:, RANK_LORA_PAD:2 * RANK_LORA_PAD].astype(BF16)
        sg_s[...] = _sigmoid(lo[:, 2 * RANK_LORA_PAD:]).astype(BF16)

    mu_p = mup_ref[...]
    mu_n = mun_ref[...]
    r = shift_mix(r_ref, rp_ref, rn_ref, mu_p[:, 0:HEAD_GROUP], mu_n[:, 0:HEAD_GROUP])
    k = shift_mix(k_ref, kp_ref, kn_ref, mu_p[:, HEAD_GROUP:2 * HEAD_GROUP], mu_n[:, HEAD_GROUP:2 * HEAD_GROUP])
    v = shift_mix(v_ref, vp_ref, vn_ref, mu_p[:, 2 * HEAD_GROUP:], mu_n[:, 2 * HEAD_GROUP:])

    ones_bd = jnp.where(_head_mask(HEAD_GROUP, HEAD_GROUP), 1.0, 0.0).astype(BF16)
    kk = k * kk_ref[...]
    kk = kk * lax.rsqrt(jnp.maximum(_head_sum(kk * kk, ones_bd), 1e-24))

    tw = tw_s[...]
    da = da_s[...]

    def direction(w0_ref, w2_ref, a0_ref, a2_ref):
        dec = w0_ref[...] + jnp.dot(tw, w2_ref[...], preferred_element_type=F32)
        z = -dec
        softplus = jnp.maximum(z, 0.0) + jnp.log1p(jnp.exp(-jnp.abs(z)))
        lw = -jnp.exp(-softplus - 0.5)
        a = _sigmoid(a0_ref[...] + jnp.dot(da, a2_ref[...], preferred_element_type=F32))
        kd = k * (1.0 + (a - 1.0) * ka_ref[...])
        return lw, a, kd

    lw_f, a_f, kd_f = direction(w0f_ref, w2f_ref, a0f_ref, a2f_ref)
    lw_b, a_b, kd_b = direction(w0b_ref, w2b_ref, a0b_ref, a2b_ref)

    bonus = _head_sum(r * (0.5 * (kd_f + kd_b)) * rk_ref[...], ones_bd) * v
    g = jnp.dot(sg_s[...], g2_ref[...], preferred_element_type=F32)

    ro_ref[0] = r.astype(ro_ref.dtype)
    vo_ref[0] = v.astype(vo_ref.dtype)
    kko_ref[0] = kk.astype(kko_ref.dtype)
    kdf_ref[0] = kd_f.astype(kdf_ref.dtype)
    kdb_ref[0] = kd_b.astype(kdb_ref.dtype)
    bf_ref[0] = (kk * a_f).astype(bf_ref.dtype)
    bb_ref[0] = (kk * a_b).astype(bb_ref.dtype)
    lwf_ref[0] = lw_f.astype(lwf_ref.dtype)
    lwb_ref[0] = lw_b.astype(lwb_ref.dtype)
    g_ref[0] = g.astype(g_ref.dtype)
    bonus_ref[0] = bonus.astype(bonus_ref.dtype)


def _rwkv_prep(rkv3, lora3, vecs, mats, tb_rows=256):
    bsz, seq, _ = rkv3.shape
    ntb = seq // tb_rows
    hgw = HEAD_GROUP
    n8 = seq // 8
    r0, k0, v0 = 0, D_RWKV // hgw, 2 * D_RWKV // hgw
    lo0 = 0

    def main(c0):
        return pl.BlockSpec((1, tb_rows, hgw), lambda b, t, h: (b, t, c0 + h))

    def prev(c0):
        return pl.BlockSpec((1, 8, hgw), lambda b, t, h: (b, jnp.maximum(t * (tb_rows // 8) - 1, 0), c0 + h))

    def nxt(c0):
        return pl.BlockSpec((1, 8, hgw), lambda b, t, h: (b, jnp.minimum((t + 1) * (tb_rows // 8), n8 - 1), c0 + h))

    lo_main = pl.BlockSpec((1, tb_rows, LORA_W), lambda b, t, h: (b, t, lo0))
    lo_prev = pl.BlockSpec((1, 8, LORA_W), lambda b, t, h: (b, jnp.maximum(t * (tb_rows // 8) - 1, 0), lo0))
    lo_next = pl.BlockSpec((1, 8, LORA_W), lambda b, t, h: (b, jnp.minimum((t + 1) * (tb_rows // 8), n8 - 1), lo0))

    def hvec():
        return pl.BlockSpec((1, hgw), lambda b, t, h: (0, h))

    def hmat(rows):
        return pl.BlockSpec((rows, hgw), lambda b, t, h: (0, h))

    mu3 = pl.BlockSpec((None, 1, 3 * hgw), lambda b, t, h: (h, 0, 0))
    full = lambda shape: pl.BlockSpec(shape, lambda b, t, h: (0,) * len(shape))

    in_specs = [main(r0), main(k0), main(v0), lo_main,
                prev(r0), prev(k0), prev(v0), lo_prev, nxt(r0), nxt(k0), nxt(v0), lo_next,
                mu3, mu3, full((1, LORA_W)), full((1, LORA_W)),
                hvec(), hvec(), hvec(), hvec(), hvec(), hvec(), hvec(),
                hmat(RANK_LORA_PAD), hmat(RANK_LORA_PAD), hmat(RANK_LORA_PAD), hmat(RANK_LORA_PAD),
                hmat(RANK_GATE)]
    out_block = pl.BlockSpec((1, tb_rows, hgw), lambda b, t, h: (b, t, h))
    shp = lambda dt: jax.ShapeDtypeStruct((bsz, seq, D_RWKV), dt)
    out_dtypes = [BF16] * 11
    return pl.pallas_call(
        functools.partial(_prep_body, tb_rows=tb_rows),
        grid=(bsz, ntb, N_HEAD_GROUPS),
        in_specs=in_specs,
        out_specs=[out_block] * len(out_dtypes),
        out_shape=[shp(dt) for dt in out_dtypes],
        scratch_shapes=[pltpu.VMEM((tb_rows, RANK_LORA_PAD), BF16),
                        pltpu.VMEM((tb_rows, RANK_LORA_PAD), BF16),
                        pltpu.VMEM((tb_rows, RANK_GATE), BF16)],
        compiler_params=_cparams(("parallel", "parallel", "arbitrary")),
        name="rwkv_prep",
    )(rkv3, rkv3, rkv3, lora3, rkv3, rkv3, rkv3, lora3, rkv3, rkv3, rkv3, lora3,
      vecs["mu_prev3"], vecs["mu_next3"], vecs["lmu_prev"], vecs["lmu_next"],
      vecs["w0f"], vecs["w0b"], vecs["a0f"], vecs["a0b"], vecs["k_k"], vecs["k_a"], vecs["r_k"],
      mats["w2f"], mats["w2b"], mats["a2f"], mats["a2b"], mats["g2"])


def _dot_nt(a, b):
    return lax.dot_general(a, b, (((1,), (1,)), ((), ())), preferred_element_type=F32)


def _block_diag(t, mask):
    return jnp.where(mask, jnp.concatenate([t, t, t, t], axis=0), jnp.zeros((), t.dtype))


def _scan_masks():
    c = CHUNK
    heads = HEAD_GROUP // RWKV_HEAD
    t_idx = lax.broadcasted_iota(jnp.int32, (c, HEAD_GROUP), 0)
    s_idx = lax.broadcasted_iota(jnp.int32, (c, HEAD_GROUP), 1) % c
    n_idx = lax.broadcasted_iota(jnp.int32, (RWKV_HEAD, HEAD_GROUP), 0)
    j_idx = lax.broadcasted_iota(jnp.int32, (RWKV_HEAD, HEAD_GROUP), 1)
    r2 = lax.broadcasted_iota(jnp.int32, (heads * 2 * c, HEAD_GROUP), 0) // (2 * c)
    c2 = lax.broadcasted_iota(jnp.int32, (heads * 2 * c, HEAD_GROUP), 1) // RWKV_HEAD
    return dict(
        bd=_head_mask(HEAD_GROUP, HEAD_GROUP),
        bd2=r2 == c2,
        before={False: s_idx < t_idx, True: s_idx > t_idx},
        upto={False: s_idx <= t_idx, True: s_idx >= t_idx},
        eye=jnp.where(s_idx == t_idx, 1.0, 0.0),
        diag=n_idx == j_idx % RWKV_HEAD)


def _scan_chunks(chains, mk):
    c = CHUNK
    bd_mask = mk["bd"]
    n = len(chains)
    rev = [ch["reverse"] for ch in chains]
    strict = [mk["before"][x] for x in rev]
    incl = [mk["upto"][x] for x in rev]
    bd = lambda t: _block_diag(t, bd_mask)
    mm = lambda a, b: jnp.dot(a, b, preferred_element_type=F32)
    each = lambda f, *cols: [f(*xs) for xs in zip(*cols)]

    tri = {x: jnp.where(mk["upto"][x][:, :c], 1.0, 0.0).astype(BF16) for x in set(rev)}
    lw = [ch["lw"] for ch in chains]
    cum = [mm(tri[x], t.astype(BF16)) for x, t in zip(rev, lw)]
    cum_edge = [t[0:1, :] if x else t[c - 1:c, :] for x, t in zip(rev, cum)]
    e_pos = each(jnp.exp, cum)
    e_neg = [jnp.exp(-t) for t in cum]
    g_edge = each(jnp.exp, cum_edge)
    e_edge = each(lambda g, e: g * e, g_edge, e_neg)
    a_t16 = [(-ch["kk"] * jnp.exp(t - l)).astype(BF16) for ch, t, l in zip(chains, cum, lw)]
    r_t = [ch["r"] * e for ch, e in zip(chains, e_pos)]
    b_t16 = [(ch["beta"] * e).astype(BF16) for ch, e in zip(chains, e_neg)]
    k_t16 = [(ch["kd"] * e).astype(BF16) for ch, e in zip(chains, e_neg)]
    b_p = [ch["beta"] * e for ch, e in zip(chains, e_edge)]
    k_p = [ch["kd"] * e for ch, e in zip(chains, e_edge)]
    v16 = [ch["v"].astype(BF16) for ch in chains]
    bd_v = each(bd, v16)

    ar = [jnp.concatenate([a, r.astype(BF16)], axis=0) for a, r in zip(a_t16, r_t)]
    pb = [_dot_nt(x, bd(b)) for x, b in zip(ar, b_t16)]
    pk = [_dot_nt(x, bd(k)) for x, k in zip(ar, k_t16)]
    p = [jnp.where(m, t[:c], 0.0) for m, t in zip(strict, pb)]
    q16 = [jnp.where(m, t[:c], 0.0).astype(BF16) for m, t in zip(strict, pk)]
    mrb16 = [jnp.where(m, t[c:], 0.0).astype(BF16) for m, t in zip(incl, pb)]
    mrk16 = [jnp.where(m, t[c:], 0.0).astype(BF16) for m, t in zip(incl, pk)]

    qv16 = [mm(a, b).astype(BF16) for a, b in zip(q16, bd_v)]

    tmat = [mk["eye"] + t for t in p]
    p16 = [t.astype(BF16) for t in p]
    pw = [mm(t, bd(t)) for t in p16]
    levels = int(math.log2(c))
    for lev in range(1, levels):
        bd_pw = [bd(t.astype(BF16)) for t in pw]
        if lev < levels - 1:
            both = [mm(jnp.concatenate([t, w], axis=0).astype(BF16), b) for t, w, b in zip(tmat, pw, bd_pw)]
            tmat = [t + x[:c] for t, x in zip(tmat, both)]
            pw = [x[c:] for x in both]
        else:
            tmat = [t + mm(t.astype(BF16), b) for t, b in zip(tmat, bd_pw)]

    t16 = [t.astype(BF16) for t in tmat]
    a16 = [mm(t, bd(a)).astype(BF16) for t, a in zip(t16, a_t16)]
    w16 = [mm(t, bd(x)).astype(BF16) for t, x in zip(t16, qv16)]

    def packed_t(b, k):
        zt = jnp.concatenate([b, k], axis=0).T
        hd = RWKV_HEAD
        return jnp.concatenate([zt[h * hd:(h + 1) * hd] for h in range(HEAD_GROUP // hd)], axis=1).astype(BF16)

    def bd2(y):
        return jnp.where(mk["bd2"], jnp.concatenate([y, y, y, y], axis=0), jnp.zeros((), y.dtype))

    zt16 = each(packed_t, b_p, k_p)
    rhs_g = [bd2(jnp.concatenate([a, jnp.zeros_like(a)], axis=0)) for a in a16]
    rhs_h = [bd2(jnp.concatenate([w, v], axis=0)) for w, v in zip(w16, v16)]
    gh = [mm(z, jnp.concatenate([g, h], axis=1)) for z, g, h in zip(zt16, rhs_g, rhs_h)]
    g_mat = [t[:, :HEAD_GROUP] + jnp.where(mk["diag"], g, 0.0) for t, g in zip(gh, g_edge)]
    h_mat = [t[:, HEAD_GROUP:] for t in gh]

    r_hat = [r + mm(m, bd(a)) for r, m, a in zip(r_t, mrb16, a16)]
    o_intra = [mm(m, bd(w)) + mm(mk_, bv) for m, w, mk_, bv in zip(mrb16, w16, mrk16, bd_v)]

    both = [mm(jnp.concatenate([r, g], axis=0).astype(BF16), bd(ch["state"].astype(BF16)))
            for r, g, ch in zip(r_hat, g_mat, chains)]
    outs = [x[:c] + o for x, o in zip(both, o_intra)]
    states = [x[c:] + h for x, h in zip(both, h_mat)]
    return outs, states


def _scan_body(rf_ref, vf_ref, kkf_ref, kdf_ref, bf_ref, lwf_ref,
               rb_ref, vb_ref, kkb_ref, kdb_ref, bb_ref, lwb_ref,
               of_ref, ob_ref, sf_ref, sb_ref, *, groups):
    @pl.when(pl.program_id(2) == 0)
    def _():
        sf_ref[...] = jnp.zeros_like(sf_ref)
        sb_ref[...] = jnp.zeros_like(sb_ref)

    mk = _scan_masks()
    chains, sinks = [], []
    for gi in range(groups):
        cols = slice(gi * HEAD_GROUP, (gi + 1) * HEAD_GROUP)
        ld = lambda ref: ref[0, :, cols].astype(F32)
        chains.append(dict(r=ld(rf_ref), v=ld(vf_ref), kk=ld(kkf_ref), kd=ld(kdf_ref), beta=ld(bf_ref),
                           lw=ld(lwf_ref), state=sf_ref[:, cols], reverse=False))
        sinks.append((of_ref, sf_ref, cols))
        chains.append(dict(r=ld(rb_ref), v=ld(vb_ref), kk=ld(kkb_ref), kd=ld(kdb_ref), beta=ld(bb_ref),
                           lw=ld(lwb_ref), state=sb_ref[:, cols], reverse=True))
        sinks.append((ob_ref, sb_ref, cols))
    outs, states = _scan_chunks(chains, mk)
    for (o_ref, s_ref, cols), o, s in zip(sinks, outs, states):
        o_ref[0, :, cols] = o.astype(o_ref.dtype)
        s_ref[:, cols] = s


def _rwkv_scan(r, v, kk, kd_f, kd_b, beta_f, beta_b, lw_f, lw_b, groups=SCAN_GROUPS):
    bsz, seq, _ = r.shape
    nc = seq // CHUNK
    width = groups * HEAD_GROUP
    fwd = pl.BlockSpec((1, CHUNK, width), lambda b, h, c: (b, c, h))
    bwd = pl.BlockSpec((1, CHUNK, width), lambda b, h, c: (b, nc - 1 - c, h))
    out = jax.ShapeDtypeStruct((bsz, seq, D_RWKV), BF16)
    return pl.pallas_call(
        functools.partial(_scan_body, groups=groups),
        grid=(bsz, D_RWKV // width, nc),
        in_specs=[fwd] * 6 + [bwd] * 6,
        out_specs=[fwd, bwd],
        out_shape=[out, out],
        scratch_shapes=[pltpu.VMEM((RWKV_HEAD, width), F32), pltpu.VMEM((RWKV_HEAD, width), F32)],
        compiler_params=_cparams(("parallel", "parallel", "arbitrary")),
        name="rwkv_scan",
    )(r, v, kk, kd_f, beta_f, lw_f, r, v, kk, kd_b, beta_b, lw_b)


def _post_body(of_ref, ob_ref, bonus_ref, g_ref, lng_ref, lnb_ref, y_ref):
    ones_bd = jnp.where(_head_mask(HEAD_GROUP, HEAD_GROUP), 1.0, 0.0).astype(BF16)
    o = of_ref[0].astype(F32) + ob_ref[0].astype(F32)
    inv_n = 1.0 / RWKV_HEAD
    mu = _head_sum(o, ones_bd) * inv_n
    d = o - mu
    var = _head_sum(d * d, ones_bd) * inv_n
    on = d * lax.rsqrt(var + EPS_GN) * lng_ref[...] + lnb_ref[...]
    y_ref[0] = ((on + bonus_ref[0].astype(F32)) * g_ref[0].astype(F32)).astype(y_ref.dtype)


def _rwkv_post(o_f, o_b, bonus, g, ln_g, ln_b, tb_rows=512):
    bsz, seq, _ = o_f.shape
    blk = pl.BlockSpec((1, tb_rows, HEAD_GROUP), lambda b, t, h: (b, t, h))
    vec = pl.BlockSpec((1, HEAD_GROUP), lambda b, t, h: (0, h))
    return pl.pallas_call(
        _post_body,
        grid=(bsz, seq // tb_rows, N_HEAD_GROUPS),
        in_specs=[blk, blk, blk, blk, vec, vec],
        out_specs=blk,
        out_shape=jax.ShapeDtypeStruct((bsz, seq, D_RWKV), BF16),
        compiler_params=_cparams(("parallel", "parallel", "parallel")),
        name="rwkv_post",
    )(o_f, o_b, bonus, g, ln_g, ln_b)


def _attn_body(qc_ref, qn_ref, k_ref, v_ref, lq1_ref, lk1_ref, lq2_ref, lk2_ref, sg_ref, o_ref,
               tbl_ref, vt_ref, t_ref, e_ref, *, tq):
    head = pl.program_id(1)
    qb = pl.program_id(2)
    nq = pl.num_programs(2)
    seq = k_ref.shape[1]
    d = DIFF_HEAD
    n_chunk = seq // ATTN_KEY_CHUNK
    n_quarter = 4
    rows_q = seq // n_quarter

    def stacked(q):
        lane = lax.broadcasted_iota(jnp.int32, (tq, 2 * d), 1)
        zero = jnp.zeros((), BF16)
        return jnp.concatenate([jnp.where(lane < d, q, zero), jnp.where(lane >= d, q, zero)], axis=0)

    def score_chunk(slot, qq, blk, c):
        rows = slice(c * ATTN_KEY_CHUNK, (c + 1) * ATTN_KEY_CHUNK)
        start = seq - tq - blk * tq + c * ATTN_KEY_CHUNK
        bias = tbl_ref[pl.ds(pl.multiple_of(start, 8), ATTN_KEY_CHUNK), :]
        t_ref[slot, rows, :] = _dot_nt(k_ref[0, rows, :], qq) + jnp.concatenate([bias, bias], axis=1)

    @pl.when(qb == 0)
    def _():
        slope = jnp.exp2(jnp.full((1, 1), -8.0 / N_DIFF_HEADS, F32) * (head + 1).astype(F32)) * LOG2E
        row = lax.broadcasted_iota(jnp.int32, tbl_ref.shape, 0)
        col = lax.broadcasted_iota(jnp.int32, tbl_ref.shape, 1)
        tbl_ref[...] = -slope * jnp.abs(col - row + (seq - tq)).astype(F32)
        vt_ref[0:2 * d, :] = v_ref[0].astype(F32).T.astype(BF16)
        ones_row = lax.broadcasted_iota(jnp.int32, (ATTN_SUM_ROWS, seq), 0) == 0
        vt_ref[2 * d:, :] = jnp.where(ones_row, 1.0, 0.0).astype(BF16)
        qq0 = stacked(qc_ref[0])
        for c in range(n_chunk):
            score_chunk(0, qq0, 0, c)

    lam = (jnp.exp(jnp.sum(lq1_ref[...] * lk1_ref[...], axis=-1, keepdims=True))
           - jnp.exp(jnp.sum(lq2_ref[...] * lk2_ref[...], axis=-1, keepdims=True)) + LAMBDA_INIT)

    blk_next = jnp.minimum(qb + 1, nq - 1)

    def step(cur, nxt):
        qq_next = stacked(qn_ref[0])
        next_chunks = iter(range(n_chunk))

        def issue_scores(count):
            for _ in range(count):
                c = next(next_chunks, None)
                if c is not None:
                    score_chunk(nxt, qq_next, blk_next, c)

        m_part = None
        for qtr in range(n_quarter):
            issue_scores(1)
            for g in range(rows_q // ATTN_ROW_GROUP):
                r0 = qtr * rows_q + g * ATTN_ROW_GROUP
                x = t_ref[cur, r0:r0 + ATTN_ROW_GROUP, :]
                for r in range(ATTN_ROW_GROUP // 8):
                    tile = x[r * 8:(r + 1) * 8, :]
                    m_part = tile if m_part is None else jnp.maximum(m_part, tile)
        m = jnp.max(m_part, axis=0, keepdims=True)

        aug = None
        for qtr in range(n_quarter):
            issue_scores(1)
            for g in range(rows_q // ATTN_ROW_GROUP):
                r0 = qtr * rows_q + g * ATTN_ROW_GROUP
                e_ref[r0:r0 + ATTN_ROW_GROUP, :] = jnp.exp2(
                    t_ref[cur, r0:r0 + ATTN_ROW_GROUP, :] - m).astype(BF16)
            rows = slice(qtr * rows_q, (qtr + 1) * rows_q)
            part = jnp.dot(vt_ref[:, rows], e_ref[rows, :], preferred_element_type=F32)
            aug = part if aug is None else aug + part
        issue_scores(n_chunk)

        acc = aug[:2 * d]
        l = aug[2 * d:2 * d + 1]
        out_t = acc[:, :tq] * (1.0 / l[:, :tq]) - acc[:, tq:] * (lam / l[:, tq:])
        out = out_t.T
        out = out * lax.rsqrt(jnp.mean(out * out, axis=-1, keepdims=True) + EPS_SUBLN) * sg_ref[...]
        o_ref[0] = (out * (1.0 - LAMBDA_INIT)).astype(o_ref.dtype)

    @pl.when(qb % 2 == 0)
    def _():
        step(0, 1)

    @pl.when(qb % 2 == 1)
    def _():
        step(1, 0)


def _diff_attn(qg3, lq1, lk1, lq2, lk2, subln_g, tq=256):
    bsz, seq, _ = qg3.shape
    w = DIFF_VDIM
    nq = seq // tq
    q0 = 0
    k0 = D_DIFF // w
    v0 = 2 * D_DIFF // w
    vec = lambda n: pl.BlockSpec((1, n), lambda b, h, i: (0, 0))
    return pl.pallas_call(
        functools.partial(_attn_body, tq=tq),
        grid=(bsz, N_DIFF_HEADS, nq),
        in_specs=[pl.BlockSpec((1, tq, w), lambda b, h, i: (b, i, q0 + h)),
                  pl.BlockSpec((1, tq, w), lambda b, h, i: (b, jnp.minimum(i + 1, nq - 1), q0 + h)),
                  pl.BlockSpec((1, seq, w), lambda b, h, i: (b, 0, k0 + h)),
                  pl.BlockSpec((1, seq, w), lambda b, h, i: (b, 0, v0 + h)),
                  vec(DIFF_HEAD), vec(DIFF_HEAD), vec(DIFF_HEAD), vec(DIFF_HEAD), vec(w)],
        out_specs=pl.BlockSpec((1, tq, w), lambda b, h, i: (b, i, h)),
        out_shape=jax.ShapeDtypeStruct((bsz, seq, D_DIFF), BF16),
        scratch_shapes=[pltpu.VMEM((2 * seq - tq, tq), F32),
                        pltpu.VMEM((w + ATTN_SUM_ROWS, seq), BF16),
                        pltpu.VMEM((2, seq, 2 * tq), F32),
                        pltpu.VMEM((seq, 2 * tq), BF16)],
        compiler_params=_cparams(("parallel", "parallel", "arbitrary")),
        name="diff_attn",
    )(qg3, qg3, qg3, qg3, lq1.reshape(1, -1), lk1.reshape(1, -1), lq2.reshape(1, -1), lk2.reshape(1, -1),
      subln_g.reshape(1, -1))


def _pad_cols(t, width):
    return jnp.pad(t, ((0, 0), (0, width - t.shape[1])))


def _pad_rows(t, rows):
    return jnp.pad(t, ((0, rows - t.shape[0]), (0, 0)))


def kernel(x, attn_pre_norm, attn_post_norm, w_in, shift_prev, shift_next, decay_bias_fwd, decay_up_fwd, decay_bias_bwd, decay_up_bwd, iclr_bias_fwd, iclr_up_fwd, iclr_bias_bwd, iclr_up_bwd, gate_up, k_k, k_a, r_k, ln_x_gain, ln_x_bias, lambda_q1, lambda_k1, lambda_q2, lambda_k2, subln_gain, w_up_rwkv, w_up_diff, w_out, mlp_pre_norm, mlp_post_norm, w_mlp_in, w_mlp_out):
    bsz, seq, d = x.shape
    m = bsz * seq
    l = 0
    x2 = x.reshape(m, d)

    wt = jnp.transpose(w_in[l])
    c_dw = 3 * D_RWKV
    c_da = c_dw + RANK_LORA
    c_dg = c_da + RANK_LORA
    c_q = c_dg + RANK_GATE
    wt_lora = jnp.concatenate([_pad_rows(wt[c_dw:c_da], RANK_LORA_PAD), _pad_rows(wt[c_da:c_dg], RANK_LORA_PAD),
                               wt[c_dg:c_q]], axis=0)

    def regroup(vec):
        t = vec.reshape(3, N_HEAD_GROUPS, HEAD_GROUP)
        return jnp.transpose(t, (1, 0, 2)).reshape(N_HEAD_GROUPS, 1, 3 * HEAD_GROUP)

    def lora_vec(vec):
        t = vec.reshape(1, -1)
        return jnp.concatenate([_pad_cols(t[:, c_dw:c_da], RANK_LORA_PAD), _pad_cols(t[:, c_da:c_dg], RANK_LORA_PAD),
                                t[:, c_dg:c_q]], axis=1)

    row = lambda t: t.reshape(1, -1)
    vecs = dict(
        mu_prev3=regroup(shift_prev[l][:c_dw]), mu_next3=regroup(shift_next[l][:c_dw]),
        lmu_prev=lora_vec(shift_prev[l]), lmu_next=lora_vec(shift_next[l]),
        w0f=row(decay_bias_fwd[l]), w0b=row(decay_bias_bwd[l]), a0f=row(iclr_bias_fwd[l]), a0b=row(iclr_bias_bwd[l]),
        k_k=row(k_k[l]), k_a=row(k_a[l]), r_k=row(r_k[l]))
    mats = dict(
        w2f=_pad_rows(decay_up_fwd[l], RANK_LORA_PAD).astype(BF16), w2b=_pad_rows(decay_up_bwd[l], RANK_LORA_PAD).astype(BF16),
        a2f=_pad_rows(iclr_up_fwd[l], RANK_LORA_PAD).astype(BF16), a2b=_pad_rows(iclr_up_bwd[l], RANK_LORA_PAD).astype(BF16),
        g2=gate_up[l].astype(BF16))

    h = _prenorm(x2, attn_pre_norm[l])
    col_scale = jnp.ones((1, N_QG), F32).at[:, :D_DIFF].set(DIFF_HEAD ** -0.5 * LOG2E)
    rkv3 = _matmul(h, wt, BF16, tm=1024, tn=512, bt_rows=(0, c_dw),
                   name="in_proj_rkv").reshape(bsz, seq, 3 * D_RWKV)
    lora3 = _matmul(h, wt_lora, BF16, tm=1024, tn=LORA_W, bt_rows=(0, LORA_W),
                    name="in_proj_lora").reshape(bsz, seq, LORA_W)
    qg = _matmul(h, wt, BF16, tm=1024, tn=512, col_scale=col_scale, bt_rows=(c_q, N_QG), name="in_proj_qg")

    r, v, kk, kd_f, kd_b, beta_f, beta_b, lw_f, lw_b, g, bonus = _rwkv_prep(rkv3, lora3, vecs, mats)
    o_f, o_b = _rwkv_scan(r, v, kk, kd_f, kd_b, beta_f, beta_b, lw_f, lw_b)
    y_a = _rwkv_post(o_f, o_b, bonus, g, row(ln_x_gain[l]), row(ln_x_bias[l]))

    y_b = _diff_attn(qg.reshape(bsz, seq, N_QG), lambda_q1[l], lambda_k1[l], lambda_q2[l], lambda_k2[l],
                     subln_gain[l])

    mixed = _upgate(y_a.reshape(m, D_RWKV), y_b.reshape(m, D_DIFF),
                    w_up_rwkv[l].astype(BF16), w_up_diff[l].astype(BF16), qg)
    z = _matmul(mixed, w_out[l], F32, tm=1024, tn=512, name="out_proj")
    x1, h2 = _resnorm(x2, z, attn_post_norm[l], mlp_pre_norm[l])

    u = _matmul(h2, w_mlp_in[l], BF16, tm=1024, tn=512, act="relu2", name="mlp_in")
    z2 = _matmul_ktiled(u, w_mlp_out[l].astype(BF16), F32, tm=1024, tn=1024, tk=2048, name="mlp_out")
    out = _resnorm(x1, z2, mlp_post_norm[l])
    return out.reshape(bsz, seq, d)
```

```python
import functools
import math

import jax
import jax.numpy as jnp
from jax import lax
from jax.experimental import pallas as pl
from jax.experimental.pallas import tpu as pltpu

F32 = jnp.float32
BF16 = jnp.bfloat16

D_MODEL = 4096
D_RWKV = D_MODEL // 2
RWKV_HEAD = 64
RANK_LORA = 96
RANK_LORA_PAD = 128
RANK_GATE = 256
D_DIFF = D_MODEL // 2
DIFF_HEAD = 64
N_DIFF_HEADS = D_DIFF // (2 * DIFF_HEAD)
DIFF_VDIM = 2 * DIFF_HEAD
D_FF = 4 * D_MODEL
EPS_RMS = 1e-6
EPS_GN = 64e-5
EPS_SUBLN = 1e-5
LAMBDA_INIT = 0.8 - 0.6 * math.exp(-0.3 * 0)
LOG2E = math.log2(math.e)

HEAD_GROUP = 256
N_HEAD_GROUPS = D_RWKV // HEAD_GROUP
LORA_W = 2 * RANK_LORA_PAD + RANK_GATE
COL_GATE = 3 * D_DIFF
N_QG = COL_GATE + 2 * D_MODEL

CHUNK = 64
SCAN_GROUPS = 8
ATTN_SUM_ROWS = 16
ATTN_KEY_CHUNK = 512
ATTN_ROW_GROUP = 32
VMEM_LIMIT = 48 * 1024 * 1024


def _cparams(sem):
    return pltpu.CompilerParams(dimension_semantics=sem, vmem_limit_bytes=VMEM_LIMIT)


def _rms(t, g):
    return t * lax.rsqrt(jnp.mean(t * t, axis=-1, keepdims=True) + EPS_RMS) * g


def _prenorm_body(x_ref, g_ref, o_ref):
    o_ref[...] = _rms(x_ref[...], g_ref[...]).astype(o_ref.dtype)


def _prenorm(x, g, tr=256):
    m, d = x.shape
    return pl.pallas_call(
        _prenorm_body,
        grid=(m // tr,),
        in_specs=[pl.BlockSpec((tr, d), lambda i: (i, 0)), pl.BlockSpec((1, d), lambda i: (0, 0))],
        out_specs=pl.BlockSpec((tr, d), lambda i: (i, 0)),
        out_shape=jax.ShapeDtypeStruct((m, d), BF16),
        compiler_params=_cparams(("parallel",)),
        name="prenorm",
    )(x, g.reshape(1, d))


def _resnorm_body(x_ref, z_ref, g_ref, g2_ref, o_ref, h_ref):
    y = x_ref[...] + _rms(z_ref[...], g_ref[...])
    o_ref[...] = y
    if h_ref is not None:
        h_ref[...] = _rms(y, g2_ref[...]).astype(h_ref.dtype)


def _resnorm(x, z, g, g_next=None, tr=256):
    m, d = x.shape
    row = pl.BlockSpec((tr, d), lambda i: (i, 0))
    vec = pl.BlockSpec((1, d), lambda i: (0, 0))
    if g_next is None:
        body = lambda x_ref, z_ref, g_ref, o_ref: _resnorm_body(x_ref, z_ref, g_ref, None, o_ref, None)
        return pl.pallas_call(
            body, grid=(m // tr,), in_specs=[row, row, vec], out_specs=row,
            out_shape=jax.ShapeDtypeStruct((m, d), F32),
            compiler_params=_cparams(("parallel",)), name="resnorm_out",
        )(x, z, g.reshape(1, d))
    return pl.pallas_call(
        _resnorm_body, grid=(m // tr,), in_specs=[row, row, vec, vec], out_specs=[row, row],
        out_shape=[jax.ShapeDtypeStruct((m, d), F32), jax.ShapeDtypeStruct((m, d), BF16)],
        compiler_params=_cparams(("parallel",)), name="resnorm_mid",
    )(x, z, g.reshape(1, d), g_next.reshape(1, d))


def _mm_body(a_ref, b_ref, *rest, act, b_transposed, scaled):
    o_ref = rest[-1]
    dims = (((1,), (1 if b_transposed else 0,)), ((), ()))
    acc = lax.dot_general(a_ref[...], b_ref[...].astype(BF16), dims, preferred_element_type=F32)
    if act == "relu2":
        acc = jnp.square(jnp.maximum(acc, 0.0))
    if scaled:
        acc = acc * rest[0][...]
    o_ref[...] = acc.astype(o_ref.dtype)


def _matmul(a, b, out_dtype, tm, tn, act=None, col_scale=None, bt_rows=None, name="matmul"):
    m, k = a.shape
    b_transposed = bt_rows is not None
    ij = lambda g0, g1: (g0, g1)
    if b_transposed:
        first, n = bt_rows
        b_block = (tn, k)
        b_spec = pl.BlockSpec((pl.Element(tn), pl.Element(k)),
                              lambda g0, g1: (pl.multiple_of(first + ij(g0, g1)[1] * tn, 16), 0))
    else:
        n = b.shape[1]
        b_block = (k, tn)
        b_spec = pl.BlockSpec(b_block, lambda g0, g1: (0, ij(g0, g1)[1]))
    in_specs = [pl.BlockSpec((tm, k), lambda g0, g1: (ij(g0, g1)[0], 0)), b_spec]
    args = (a, b)
    if col_scale is not None:
        in_specs.append(pl.BlockSpec((1, tn), lambda g0, g1: (0, ij(g0, g1)[1])))
        args = (a, b, col_scale)
    body = functools.partial(_mm_body, act=act, b_transposed=b_transposed, scaled=col_scale is not None)
    return pl.pallas_call(
        body,
        grid=(m // tm, n // tn),
        in_specs=in_specs,
        out_specs=pl.BlockSpec((tm, tn), lambda g0, g1: ij(g0, g1)),
        out_shape=jax.ShapeDtypeStruct((m, n), out_dtype),
        compiler_params=_cparams(("parallel", "arbitrary")),
        name=name,
    )(*args)


def _mmk_body(a_ref, b_ref, o_ref, acc_ref):
    kk = pl.program_id(2)

    @pl.when(kk == 0)
    def _():
        acc_ref[...] = jnp.zeros_like(acc_ref)

    acc_ref[...] += jnp.dot(a_ref[...], b_ref[...].astype(BF16), preferred_element_type=F32)

    @pl.when(kk == pl.num_programs(2) - 1)
    def _():
        o_ref[...] = acc_ref[...].astype(o_ref.dtype)


def _matmul_ktiled(a, b, out_dtype, tm, tn, tk, name="matmul_k"):
    m, k = a.shape
    _, n = b.shape
    return pl.pallas_call(
        _mmk_body,
        grid=(m // tm, n // tn, k // tk),
        in_specs=[pl.BlockSpec((tm, tk), lambda i, j, q: (i, q)), pl.BlockSpec((tk, tn), lambda i, j, q: (q, j))],
        out_specs=pl.BlockSpec((tm, tn), lambda i, j, q: (i, j)),
        out_shape=jax.ShapeDtypeStruct((m, n), out_dtype),
        scratch_shapes=[pltpu.VMEM((tm, tn), F32)],
        compiler_params=_cparams(("parallel", "parallel", "arbitrary")),
        name=name,
    )(a, b)


def _sigmoid(t):
    return 1.0 / (1.0 + jnp.exp(-t))


def _upgate_body(ya_ref, yb_ref, wa_ref, wb_ref, ga_ref, gb_ref, o_ref):
    acc_a = jnp.dot(ya_ref[...], wa_ref[...].astype(BF16), preferred_element_type=F32)
    acc_b = jnp.dot(yb_ref[...], wb_ref[...].astype(BF16), preferred_element_type=F32)
    ga = _sigmoid(ga_ref[...].astype(F32))
    gb = _sigmoid(gb_ref[...].astype(F32))
    o_ref[...] = (ga * acc_a + gb * acc_b).astype(o_ref.dtype)


def _upgate(ya, yb, wa, wb, qg, tm=1024, tn=512):
    m, k = ya.shape
    n = wa.shape[1]
    ga0 = COL_GATE // tn
    gb0 = (COL_GATE + D_MODEL) // tn
    return pl.pallas_call(
        _upgate_body,
        grid=(m // tm, n // tn),
        in_specs=[
            pl.BlockSpec((tm, k), lambda i, j: (i, 0)),
            pl.BlockSpec((tm, k), lambda i, j: (i, 0)),
            pl.BlockSpec((k, tn), lambda i, j: (0, j)),
            pl.BlockSpec((k, tn), lambda i, j: (0, j)),
            pl.BlockSpec((tm, tn), lambda i, j: (i, ga0 + j)),
            pl.BlockSpec((tm, tn), lambda i, j: (i, gb0 + j)),
        ],
        out_specs=pl.BlockSpec((tm, tn), lambda i, j: (i, j)),
        out_shape=jax.ShapeDtypeStruct((m, n), BF16),
        compiler_params=_cparams(("parallel", "arbitrary")),
        name="upgate",
    )(ya, yb, wa, wb, qg, qg)


def _head_mask(rows, cols):
    r = lax.broadcasted_iota(jnp.int32, (rows, cols), 0) // RWKV_HEAD
    c = lax.broadcasted_iota(jnp.int32, (rows, cols), 1) // RWKV_HEAD
    return r == c


def _split_bf16(t):
    hi = t.astype(BF16)
    lo = (t - hi.astype(F32)).astype(BF16)
    return hi, lo


def _head_sum(t, ones_bd):
    hi, lo = _split_bf16(t)
    return (jnp.dot(hi, ones_bd, preferred_element_type=F32)
            + jnp.dot(lo, ones_bd, preferred_element_type=F32))


def _prep_body(r_ref, k_ref, v_ref, lo_ref,
               rp_ref, kp_ref, vp_ref, lop_ref, rn_ref, kn_ref, vn_ref, lon_ref,
               mup_ref, mun_ref, lmup_ref, lmun_ref,
               w0f_ref, w0b_ref, a0f_ref, a0b_ref, kk_ref, ka_ref, rk_ref,
               w2f_ref, w2b_ref, a2f_ref, a2b_ref, g2_ref,
               ro_ref, vo_ref, kko_ref, kdf_ref, kdb_ref, bf_ref, bb_ref, lwf_ref, lwb_ref,
               g_ref, bonus_ref,
               tw_s, da_s, sg_s, *, tb_rows):
    tb = pl.program_id(1)
    hg = pl.program_id(2)
    first = tb == 0
    last = tb == pl.num_programs(1) - 1

    def shift_mix(x_ref, p_ref, n_ref, mup, mun):
        x = x_ref[0].astype(F32)
        width = x.shape[1]
        prev_row = jnp.where(first, 0.0, p_ref[0][7:8, :].astype(F32))
        next_row = jnp.where(last, 0.0, n_ref[0][0:1, :].astype(F32))
        row = lax.broadcasted_iota(jnp.int32, (tb_rows, width), 0)
        xp = jnp.where(row == 0, prev_row, pltpu.roll(x, 1, axis=0))
        xn = jnp.where(row == tb_rows - 1, next_row, pltpu.roll(x, tb_rows - 1, axis=0))
        return x + mup * (xp - x) + mun * (xn - x)

    @pl.when(hg == 0)
    def _():
        lo = shift_mix(lo_ref, lop_ref, lon_ref, lmup_ref[...], lmun_ref[...])
        tw_s[...] = jnp.tanh(lo[:, :RANK_LORA_PAD]).astype(BF16)
        da_s[...] = lo[:, RANK_LORA_PAD:2 * RANK_LORA_PAD].astype(BF16)
        sg_s[...] = _sigmoid(lo[:, 2 * RANK_LORA_PAD:]).astype(BF16)

    mu_p = mup_ref[...]
    mu_n = mun_ref[...]
    r = shift_mix(r_ref, rp_ref, rn_ref, mu_p[:, 0:HEAD_GROUP], mu_n[:, 0:HEAD_GROUP])
    k = shift_mix(k_ref, kp_ref, kn_ref, mu_p[:, HEAD_GROUP:2 * HEAD_GROUP], mu_n[:, HEAD_GROUP:2 * HEAD_GROUP])
    v = shift_mix(v_ref, vp_ref, vn_ref, mu_p[:, 2 * HEAD_GROUP:], mu_n[:, 2 * HEAD_GROUP:])

    ones_bd = jnp.where(_head_mask(HEAD_GROUP, HEAD_GROUP), 1.0, 0.0).astype(BF16)
    kk = k * kk_ref[...]
    kk = kk * lax.rsqrt(jnp.maximum(_head_sum(kk * kk, ones_bd), 1e-24))

    tw = tw_s[...]
    da = da_s[...]

    def direction(w0_ref, w2_ref, a0_ref, a2_ref):
        dec = w0_ref[...] + jnp.dot(tw, w2_ref[...], preferred_element_type=F32)
        z = -dec
        softplus = jnp.maximum(z, 0.0) + jnp.log1p(jnp.exp(-jnp.abs(z)))
        lw = -jnp.exp(-softplus - 0.5)
        a = _sigmoid(a0_ref[...] + jnp.dot(da, a2_ref[...], preferred_element_type=F32))
        kd = k * (1.0 + (a - 1.0) * ka_ref[...])
        return lw, a, kd

    lw_f, a_f, kd_f = direction(w0f_ref, w2f_ref, a0f_ref, a2f_ref)
    lw_b, a_b, kd_b = direction(w0b_ref, w2b_ref, a0b_ref, a2b_ref)

    bonus = _head_sum(r * (0.5 * (kd_f + kd_b)) * rk_ref[...], ones_bd) * v
    g = jnp.dot(sg_s[...], g2_ref[...], preferred_element_type=F32)

    ro_ref[0] = r.astype(ro_ref.dtype)
    vo_ref[0] = v.astype(vo_ref.dtype)
    kko_ref[0] = kk.astype(kko_ref.dtype)
    kdf_ref[0] = kd_f.astype(kdf_ref.dtype)
    kdb_ref[0] = kd_b.astype(kdb_ref.dtype)
    bf_ref[0] = (kk * a_f).astype(bf_ref.dtype)
    bb_ref[0] = (kk * a_b).astype(bb_ref.dtype)
    lwf_ref[0] = lw_f.astype(lwf_ref.dtype)
    lwb_ref[0] = lw_b.astype(lwb_ref.dtype)
    g_ref[0] = g.astype(g_ref.dtype)
    bonus_ref[0] = bonus.astype(bonus_ref.dtype)


def _rwkv_prep(rkv3, lora3, vecs, mats, tb_rows=2048):
    bsz, seq, _ = rkv3.shape
    tb_rows = min(tb_rows, seq)
    ntb = seq // tb_rows
    hgw = HEAD_GROUP
    n8 = seq // 8
    r0, k0, v0 = 0, D_RWKV // hgw, 2 * D_RWKV // hgw
    lo0 = 0

    def main(c0):
        return pl.BlockSpec((1, tb_rows, hgw), lambda b, t, h: (b, t, c0 + h))

    def prev(c0):
        return pl.BlockSpec((1, 8, hgw), lambda b, t, h: (b, jnp.maximum(t * (tb_rows // 8) - 1, 0), c0 + h))

    def nxt(c0):
        return pl.BlockSpec((1, 8, hgw), lambda b, t, h: (b, jnp.minimum((t + 1) * (tb_rows // 8), n8 - 1), c0 + h))

    lo_main = pl.BlockSpec((1, tb_rows, LORA_W), lambda b, t, h: (b, t, lo0))
    lo_prev = pl.BlockSpec((1, 8, LORA_W), lambda b, t, h: (b, jnp.maximum(t * (tb_rows // 8) - 1, 0), lo0))
    lo_next = pl.BlockSpec((1, 8, LORA_W), lambda b, t, h: (b, jnp.minimum((t + 1) * (tb_rows // 8), n8 - 1), lo0))

    def hvec():
        return pl.BlockSpec((1, hgw), lambda b, t, h: (0, h))

    def hmat(rows):
        return pl.BlockSpec((rows, hgw), lambda b, t, h: (0, h))

    mu3 = pl.BlockSpec((None, 1, 3 * hgw), lambda b, t, h: (h, 0, 0))
    full = lambda shape: pl.BlockSpec(shape, lambda b, t, h: (0,) * len(shape))

    in_specs = [main(r0), main(k0), main(v0), lo_main,
                prev(r0), prev(k0), prev(v0), lo_prev, nxt(r0), nxt(k0), nxt(v0), lo_next,
                mu3, mu3, full((1, LORA_W)), full((1, LORA_W)),
                hvec(), hvec(), hvec(), hvec(), hvec(), hvec(), hvec(),
                hmat(RANK_LORA_PAD), hmat(RANK_LORA_PAD), hmat(RANK_LORA_PAD), hmat(RANK_LORA_PAD),
                hmat(RANK_GATE)]
    out_block = pl.BlockSpec((1, tb_rows, hgw), lambda b, t, h: (b, t, h))
    shp = lambda dt: jax.ShapeDtypeStruct((bsz, seq, D_RWKV), dt)
    out_dtypes = [BF16] * 11
    return pl.pallas_call(
        functools.partial(_prep_body, tb_rows=tb_rows),
        grid=(bsz, ntb, N_HEAD_GROUPS),
        in_specs=in_specs,
        out_specs=[out_block] * len(out_dtypes),
        out_shape=[shp(dt) for dt in out_dtypes],
        scratch_shapes=[pltpu.VMEM((tb_rows, RANK_LORA_PAD), BF16),
                        pltpu.VMEM((tb_rows, RANK_LORA_PAD), BF16),
                        pltpu.VMEM((tb_rows, RANK_GATE), BF16)],
        compiler_params=_cparams(("parallel", "parallel", "arbitrary")),
        name="rwkv_prep",
    )(rkv3, rkv3, rkv3, lora3, rkv3, rkv3, rkv3, lora3, rkv3, rkv3, rkv3, lora3,
      vecs["mu_prev3"], vecs["mu_next3"], vecs["lmu_prev"], vecs["lmu_next"],
      vecs["w0f"], vecs["w0b"], vecs["a0f"], vecs["a0b"], vecs["k_k"], vecs["k_a"], vecs["r_k"],
      mats["w2f"], mats["w2b"], mats["a2f"], mats["a2b"], mats["g2"])


def _dot_nt(a, b):
    return lax.dot_general(a, b, (((1,), (1,)), ((), ())), preferred_element_type=F32)


def _block_diag(t, mask):
    return jnp.where(mask, jnp.concatenate([t, t, t, t], axis=0), jnp.zeros((), t.dtype))


def _scan_masks():
    c = CHUNK
    heads = HEAD_GROUP // RWKV_HEAD
    t_idx = lax.broadcasted_iota(jnp.int32, (c, HEAD_GROUP), 0)
    s_idx = lax.broadcasted_iota(jnp.int32, (c, HEAD_GROUP), 1) % c
    n_idx = lax.broadcasted_iota(jnp.int32, (RWKV_HEAD, HEAD_GROUP), 0)
    j_idx = lax.broadcasted_iota(jnp.int32, (RWKV_HEAD, HEAD_GROUP), 1)
    r2 = lax.broadcasted_iota(jnp.int32, (heads * 2 * c, HEAD_GROUP), 0) // (2 * c)
    c2 = lax.broadcasted_iota(jnp.int32, (heads * 2 * c, HEAD_GROUP), 1) // RWKV_HEAD
    return dict(
        bd=_head_mask(HEAD_GROUP, HEAD_GROUP),
        bd2=r2 == c2,
        before={False: s_idx < t_idx, True: s_idx > t_idx},
        upto={False: s_idx <= t_idx, True: s_idx >= t_idx},
        eye=jnp.where(s_idx == t_idx, 1.0, 0.0),
        diag=n_idx == j_idx % RWKV_HEAD)


def _scan_chunks(chains, mk):
    c = CHUNK
    bd_mask = mk["bd"]
    n = len(chains)
    rev = [ch["reverse"] for ch in chains]
    strict = [mk["before"][x] for x in rev]
    incl = [mk["upto"][x] for x in rev]
    bd = lambda t: _block_diag(t, bd_mask)
    mm = lambda a, b: jnp.dot(a, b, preferred_element_type=F32)
    each = lambda f, *cols: [f(*xs) for xs in zip(*cols)]

    tri = {x: jnp.where(mk["upto"][x][:, :c], 1.0, 0.0).astype(BF16) for x in set(rev)}
    lw = [ch["lw"] for ch in chains]
    cum = [mm(tri[x], t.astype(BF16)) for x, t in zip(rev, lw)]
    cum_edge = [t[0:1, :] if x else t[c - 1:c, :] for x, t in zip(rev, cum)]
    e_pos = each(jnp.exp, cum)
    e_neg = [jnp.exp(-t) for t in cum]
    g_edge = each(jnp.exp, cum_edge)
    e_edge = each(lambda g, e: g * e, g_edge, e_neg)
    a_t16 = [(-ch["kk"] * jnp.exp(t - l)).astype(BF16) for ch, t, l in zip(chains, cum, lw)]
    r_t = [ch["r"] * e for ch, e in zip(chains, e_pos)]
    b_t16 = [(ch["beta"] * e).astype(BF16) for ch, e in zip(chains, e_neg)]
    k_t16 = [(ch["kd"] * e).astype(BF16) for ch, e in zip(chains, e_neg)]
    b_p = [ch["beta"] * e for ch, e in zip(chains, e_edge)]
    k_p = [ch["kd"] * e for ch, e in zip(chains, e_edge)]
    v16 = [ch["v"].astype(BF16) for ch in chains]
    bd_v = each(bd, v16)

    ar = [jnp.concatenate([a, r.astype(BF16)], axis=0) for a, r in zip(a_t16, r_t)]
    pb = [_dot_nt(x, bd(b)) for x, b in zip(ar, b_t16)]
    pk = [_dot_nt(x, bd(k)) for x, k in zip(ar, k_t16)]
    p = [jnp.where(m, t[:c], 0.0) for m, t in zip(strict, pb)]
    q16 = [jnp.where(m, t[:c], 0.0).astype(BF16) for m, t in zip(strict, pk)]
    mrb16 = [jnp.where(m, t[c:], 0.0).astype(BF16) for m, t in zip(incl, pb)]
    mrk16 = [jnp.where(m, t[c:], 0.0).astype(BF16) for m, t in zip(incl, pk)]

    qv16 = [mm(a, b).astype(BF16) for a, b in zip(q16, bd_v)]

    tmat = [mk["eye"] + t for t in p]
    p16 = [t.astype(BF16) for t in p]
    pw = [mm(t, bd(t)) for t in p16]
    levels = int(math.log2(c))
    for lev in range(1, levels):
        bd_pw = [bd(t.astype(BF16)) for t in pw]
        if lev < levels - 1:
            both = [mm(jnp.concatenate([t, w], axis=0).astype(BF16), b) for t, w, b in zip(tmat, pw, bd_pw)]
            tmat = [t + x[:c] for t, x in zip(tmat, both)]
            pw = [x[c:] for x in both]
        else:
            tmat = [t + mm(t.astype(BF16), b) for t, b in zip(tmat, bd_pw)]

    t16 = [t.astype(BF16) for t in tmat]
    a16 = [mm(t, bd(a)).astype(BF16) for t, a in zip(t16, a_t16)]
    w16 = [mm(t, bd(x)).astype(BF16) for t, x in zip(t16, qv16)]

    def packed_t(b, k):
        zt = jnp.concatenate([b, k], axis=0).T
        hd = RWKV_HEAD
        return jnp.concatenate([zt[h * hd:(h + 1) * hd] for h in range(HEAD_GROUP // hd)], axis=1).astype(BF16)

    def bd2(y):
        return jnp.where(mk["bd2"], jnp.concatenate([y, y, y, y], axis=0), jnp.zeros((), y.dtype))

    zt16 = each(packed_t, b_p, k_p)
    rhs_g = [bd2(jnp.concatenate([a, jnp.zeros_like(a)], axis=0)) for a in a16]
    rhs_h = [bd2(jnp.concatenate([w, v], axis=0)) for w, v in zip(w16, v16)]
    gh = [mm(z, jnp.concatenate([g, h], axis=1)) for z, g, h in zip(zt16, rhs_g, rhs_h)]
    g_mat = [t[:, :HEAD_GROUP] + jnp.where(mk["diag"], g, 0.0) for t, g in zip(gh, g_edge)]
    h_mat = [t[:, HEAD_GROUP:] for t in gh]

    r_hat = [r + mm(m, bd(a)) for r, m, a in zip(r_t, mrb16, a16)]
    o_intra = [mm(m, bd(w)) + mm(mk_, bv) for m, w, mk_, bv in zip(mrb16, w16, mrk16, bd_v)]

    both = [mm(jnp.concatenate([r, g], axis=0).astype(BF16), bd(ch["state"].astype(BF16)))
            for r, g, ch in zip(r_hat, g_mat, chains)]
    outs = [x[:c] + o for x, o in zip(both, o_intra)]
    states = [x[c:] + h for x, h in zip(both, h_mat)]
    return outs, states


def _scan_body(rf_ref, vf_ref, kkf_ref, kdf_ref, bf_ref, lwf_ref,
               rb_ref, vb_ref, kkb_ref, kdb_ref, bb_ref, lwb_ref,
               of_ref, ob_ref, sf_ref, sb_ref, *, groups):
    @pl.when(pl.program_id(2) == 0)
    def _():
        sf_ref[...] = jnp.zeros_like(sf_ref)
        sb_ref[...] = jnp.zeros_like(sb_ref)

    mk = _scan_masks()
    chains, sinks = [], []
    for gi in range(groups):
        cols = slice(gi * HEAD_GROUP, (gi + 1) * HEAD_GROUP)
        ld = lambda ref: ref[0, :, cols].astype(F32)
        chains.append(dict(r=ld(rf_ref), v=ld(vf_ref), kk=ld(kkf_ref), kd=ld(kdf_ref), beta=ld(bf_ref),
                           lw=ld(lwf_ref), state=sf_ref[:, cols], reverse=False))
        sinks.append((of_ref, sf_ref, cols))
        chains.append(dict(r=ld(rb_ref), v=ld(vb_ref), kk=ld(kkb_ref), kd=ld(kdb_ref), beta=ld(bb_ref),
                           lw=ld(lwb_ref), state=sb_ref[:, cols], reverse=True))
        sinks.append((ob_ref, sb_ref, cols))
    outs, states = _scan_chunks(chains, mk)
    for (o_ref, s_ref, cols), o, s in zip(sinks, outs, states):
        o_ref[0, :, cols] = o.astype(o_ref.dtype)
        s_ref[:, cols] = s


def _rwkv_scan(r, v, kk, kd_f, kd_b, beta_f, beta_b, lw_f, lw_b, groups=SCAN_GROUPS):
    bsz, seq, _ = r.shape
    nc = seq // CHUNK
    width = groups * HEAD_GROUP
    fwd = pl.BlockSpec((1, CHUNK, width), lambda b, h, c: (b, c, h))
    bwd = pl.BlockSpec((1, CHUNK, width), lambda b, h, c: (b, nc - 1 - c, h))
    out = jax.ShapeDtypeStruct((bsz, seq, D_RWKV), BF16)
    return pl.pallas_call(
        functools.partial(_scan_body, groups=groups),
        grid=(bsz, D_RWKV // width, nc),
        in_specs=[fwd] * 6 + [bwd] * 6,
        out_specs=[fwd, bwd],
        out_shape=[out, out],
        scratch_shapes=[pltpu.VMEM((RWKV_HEAD, width), F32), pltpu.VMEM((RWKV_HEAD, width), F32)],
        compiler_params=_cparams(("parallel", "parallel", "arbitrary")),
        name="rwkv_scan",
    )(r, v, kk, kd_f, beta_f, lw_f, r, v, kk, kd_b, beta_b, lw_b)


def _post_body(of_ref, ob_ref, bonus_ref, g_ref, lng_ref, lnb_ref, y_ref):
    ones_bd = jnp.where(_head_mask(HEAD_GROUP, HEAD_GROUP), 1.0, 0.0).astype(BF16)
    o = of_ref[0].astype(F32) + ob_ref[0].astype(F32)
    inv_n = 1.0 / RWKV_HEAD
    mu = _head_sum(o, ones_bd) * inv_n
    d = o - mu
    var = _head_sum(d * d, ones_bd) * inv_n
    on = d * lax.rsqrt(var + EPS_GN) * lng_ref[...] + lnb_ref[...]
    y_ref[0] = ((on + bonus_ref[0].astype(F32)) * g_ref[0].astype(F32)).astype(y_ref.dtype)


def _rwkv_post(o_f, o_b, bonus, g, ln_g, ln_b, tb_rows=4096):
    bsz, seq, _ = o_f.shape
    tb_rows = min(tb_rows, seq)
    blk = pl.BlockSpec((1, tb_rows, HEAD_GROUP), lambda b, t, h: (b, t, h))
    vec = pl.BlockSpec((1, HEAD_GROUP), lambda b, t, h: (0, h))
    return pl.pallas_call(
        _post_body,
        grid=(bsz, seq // tb_rows, N_HEAD_GROUPS),
        in_specs=[blk, blk, blk, blk, vec, vec],
        out_specs=blk,
        out_shape=jax.ShapeDtypeStruct((bsz, seq, D_RWKV), BF16),
        compiler_params=_cparams(("parallel", "parallel", "parallel")),
        name="rwkv_post",
    )(o_f, o_b, bonus, g, ln_g, ln_b)


def _attn_body(qc_ref, qn_ref, k_ref, v_ref, lq1_ref, lk1_ref, lq2_ref, lk2_ref, sg_ref, o_ref,
               tbl_ref, vt_ref, t_ref, e_ref, *, tq):
    head = pl.program_id(1)
    qb = pl.program_id(2)
    nq = pl.num_programs(2)
    seq = k_ref.shape[1]
    d = DIFF_HEAD
    n_chunk = seq // ATTN_KEY_CHUNK
    n_quarter = 4
    rows_q = seq // n_quarter

    def stacked(q):
        lane = lax.broadcasted_iota(jnp.int32, (tq, 2 * d), 1)
        zero = jnp.zeros((), BF16)
        return jnp.concatenate([jnp.where(lane < d, q, zero), jnp.where(lane >= d, q, zero)], axis=0)

    def score_chunk(slot, qq, blk, c):
        rows = slice(c * ATTN_KEY_CHUNK, (c + 1) * ATTN_KEY_CHUNK)
        start = seq - tq - blk * tq + c * ATTN_KEY_CHUNK
        bias = tbl_ref[pl.ds(pl.multiple_of(start, 8), ATTN_KEY_CHUNK), :]
        t_ref[slot, rows, :] = _dot_nt(k_ref[0, rows, :], qq) + jnp.concatenate([bias, bias], axis=1)

    @pl.when(qb == 0)
    def _():
        slope = jnp.exp2(jnp.full((1, 1), -8.0 / N_DIFF_HEADS, F32) * (head + 1).astype(F32)) * LOG2E
        row = lax.broadcasted_iota(jnp.int32, tbl_ref.shape, 0)
        col = lax.broadcasted_iota(jnp.int32, tbl_ref.shape, 1)
        tbl_ref[...] = -slope * jnp.abs(col - row + (seq - tq)).astype(F32)
        vt_ref[0:2 * d, :] = v_ref[0].astype(F32).T.astype(BF16)
        ones_row = lax.broadcasted_iota(jnp.int32, (ATTN_SUM_ROWS, seq), 0) == 0
        vt_ref[2 * d:, :] = jnp.where(ones_row, 1.0, 0.0).astype(BF16)
        qq0 = stacked(qc_ref[0])
        for c in range(n_chunk):
            score_chunk(0, qq0, 0, c)

    lam = (jnp.exp(jnp.sum(lq1_ref[...] * lk1_ref[...], axis=-1, keepdims=True))
           - jnp.exp(jnp.sum(lq2_ref[...] * lk2_ref[...], axis=-1, keepdims=True)) + LAMBDA_INIT)

    blk_next = jnp.minimum(qb + 1, nq - 1)

    def step(cur, nxt):
        qq_next = stacked(qn_ref[0])
        next_chunks = iter(range(n_chunk))

        def issue_scores(count):
            for _ in range(count):
                c = next(next_chunks, None)
                if c is not None:
                    score_chunk(nxt, qq_next, blk_next, c)

        m_part = None
        for qtr in range(n_quarter):
            issue_scores(1)
            for g in range(rows_q // ATTN_ROW_GROUP):
                r0 = qtr * rows_q + g * ATTN_ROW_GROUP
                x = t_ref[cur, r0:r0 + ATTN_ROW_GROUP, :]
                for r in range(ATTN_ROW_GROUP // 8):
                    tile = x[r * 8:(r + 1) * 8, :]
                    m_part = tile if m_part is None else jnp.maximum(m_part, tile)
        m = jnp.max(m_part, axis=0, keepdims=True)

        aug = None
        for qtr in range(n_quarter):
            issue_scores(1)
            for g in range(rows_q // ATTN_ROW_GROUP):
                r0 = qtr * rows_q + g * ATTN_ROW_GROUP
                e_ref[r0:r0 + ATTN_ROW_GROUP, :] = jnp.exp2(
                    t_ref[cur, r0:r0 + ATTN_ROW_GROUP, :] - m).astype(BF16)
            rows = slice(qtr * rows_q, (qtr + 1) * rows_q)
            part = jnp.dot(vt_ref[:, rows], e_ref[rows, :], preferred_element_type=F32)
            aug = part if aug is None else aug + part
        issue_scores(n_chunk)

        acc = aug[:2 * d]
        l = aug[2 * d:2 * d + 1]
        out_t = acc[:, :tq] * (1.0 / l[:, :tq]) - acc[:, tq:] * (lam / l[:, tq:])
        out = out_t.T
        out = out * lax.rsqrt(jnp.mean(out * out, axis=-1, keepdims=True) + EPS_SUBLN) * sg_ref[...]
        o_ref[0] = (out * (1.0 - LAMBDA_INIT)).astype(o_ref.dtype)

    @pl.when(qb % 2 == 0)
    def _():
        step(0, 1)

    @pl.when(qb % 2 == 1)
    def _():
        step(1, 0)


def _diff_attn(qg3, lq1, lk1, lq2, lk2, subln_g, tq=256):
    bsz, seq, _ = qg3.shape
    w = DIFF_VDIM
    nq = seq // tq
    q0 = 0
    k0 = D_DIFF // w
    v0 = 2 * D_DIFF // w
    vec = lambda n: pl.BlockSpec((1, n), lambda b, h, i: (0, 0))
    return pl.pallas_call(
        functools.partial(_attn_body, tq=tq),
        grid=(bsz, N_DIFF_HEADS, nq),
        in_specs=[pl.BlockSpec((1, tq, w), lambda b, h, i: (b, i, q0 + h)),
                  pl.BlockSpec((1, tq, w), lambda b, h, i: (b, jnp.minimum(i + 1, nq - 1), q0 + h)),
                  pl.BlockSpec((1, seq, w), lambda b, h, i: (b, 0, k0 + h)),
                  pl.BlockSpec((1, seq, w), lambda b, h, i: (b, 0, v0 + h)),
                  vec(DIFF_HEAD), vec(DIFF_HEAD), vec(DIFF_HEAD), vec(DIFF_HEAD), vec(w)],
        out_specs=pl.BlockSpec((1, tq, w), lambda b, h, i: (b, i, h)),
        out_shape=jax.ShapeDtypeStruct((bsz, seq, D_DIFF), BF16),
        scratch_shapes=[pltpu.VMEM((2 * seq - tq, tq), F32),
                        pltpu.VMEM((w + ATTN_SUM_ROWS, seq), BF16),
                        pltpu.VMEM((2, seq, 2 * tq), F32),
                        pltpu.VMEM((seq, 2 * tq), BF16)],
        compiler_params=_cparams(("parallel", "parallel", "arbitrary")),
        name="diff_attn",
    )(qg3, qg3, qg3, qg3, lq1.reshape(1, -1), lk1.reshape(1, -1), lq2.reshape(1, -1), lk2.reshape(1, -1),
      subln_g.reshape(1, -1))


def _pad_cols(t, width):
    return jnp.pad(t, ((0, 0), (0, width - t.shape[1])))


def _pad_rows(t, rows):
    return jnp.pad(t, ((0, rows - t.shape[0]), (0, 0)))


def kernel(x, attn_pre_norm, attn_post_norm, w_in, shift_prev, shift_next, decay_bias_fwd, decay_up_fwd, decay_bias_bwd, decay_up_bwd, iclr_bias_fwd, iclr_up_fwd, iclr_bias_bwd, iclr_up_bwd, gate_up, k_k, k_a, r_k, ln_x_gain, ln_x_bias, lambda_q1, lambda_k1, lambda_q2, lambda_k2, subln_gain, w_up_rwkv, w_up_diff, w_out, mlp_pre_norm, mlp_post_norm, w_mlp_in, w_mlp_out):
    bsz, seq, d = x.shape
    m = bsz * seq
    l = 0
    x2 = x.reshape(m, d)

    wt = jnp.transpose(w_in[l])
    c_dw = 3 * D_RWKV
    c_da = c_dw + RANK_LORA
    c_dg = c_da + RANK_LORA
    c_q = c_dg + RANK_GATE
    wt_lora = jnp.concatenate([_pad_rows(wt[c_dw:c_da], RANK_LORA_PAD), _pad_rows(wt[c_da:c_dg], RANK_LORA_PAD),
                               wt[c_dg:c_q]], axis=0)

    def regroup(vec):
        t = vec.reshape(3, N_HEAD_GROUPS, HEAD_GROUP)
        return jnp.transpose(t, (1, 0, 2)).reshape(N_HEAD_GROUPS, 1, 3 * HEAD_GROUP)

    def lora_vec(vec):
        t = vec.reshape(1, -1)
        return jnp.concatenate([_pad_cols(t[:, c_dw:c_da], RANK_LORA_PAD), _pad_cols(t[:, c_da:c_dg], RANK_LORA_PAD),
                                t[:, c_dg:c_q]], axis=1)

    row = lambda t: t.reshape(1, -1)
    vecs = dict(
        mu_prev3=regroup(shift_prev[l][:c_dw]), mu_next3=regroup(shift_next[l][:c_dw]),
        lmu_prev=lora_vec(shift_prev[l]), lmu_next=lora_vec(shift_next[l]),
        w0f=row(decay_bias_fwd[l]), w0b=row(decay_bias_bwd[l]), a0f=row(iclr_bias_fwd[l]), a0b=row(iclr_bias_bwd[l]),
        k_k=row(k_k[l]), k_a=row(k_a[l]), r_k=row(r_k[l]))
    mats = dict(
        w2f=_pad_rows(decay_up_fwd[l], RANK_LORA_PAD).astype(BF16), w2b=_pad_rows(decay_up_bwd[l], RANK_LORA_PAD).astype(BF16),
        a2f=_pad_rows(iclr_up_fwd[l], RANK_LORA_PAD).astype(BF16), a2b=_pad_rows(iclr_up_bwd[l], RANK_LORA_PAD).astype(BF16),
        g2=gate_up[l].astype(BF16))

    h = _prenorm(x2, attn_pre_norm[l])
    col_scale = jnp.ones((1, N_QG), F32).at[:, :D_DIFF].set(DIFF_HEAD ** -0.5 * LOG2E)
    rkv3 = _matmul(h, wt, BF16, tm=1024, tn=512, bt_rows=(0, c_dw),
                   name="in_proj_rkv").reshape(bsz, seq, 3 * D_RWKV)
    lora3 = _matmul(h, wt_lora, BF16, tm=1024, tn=LORA_W, bt_rows=(0, LORA_W),
                    name="in_proj_lora").reshape(bsz, seq, LORA_W)
    qg = _matmul(h, wt, BF16, tm=1024, tn=512, col_scale=col_scale, bt_rows=(c_q, N_QG), name="in_proj_qg")

    r, v, kk, kd_f, kd_b, beta_f, beta_b, lw_f, lw_b, g, bonus = _rwkv_prep(rkv3, lora3, vecs, mats)
    o_f, o_b = _rwkv_scan(r, v, kk, kd_f, kd_b, beta_f, beta_b, lw_f, lw_b)
    y_a = _rwkv_post(o_f, o_b, bonus, g, row(ln_x_gain[l]), row(ln_x_bias[l]))

    y_b = _diff_attn(qg.reshape(bsz, seq, N_QG), lambda_q1[l], lambda_k1[l], lambda_q2[l], lambda_k2[l],
                     subln_gain[l])

    mixed = _upgate(y_a.reshape(m, D_RWKV), y_b.reshape(m, D_DIFF),
                    w_up_rwkv[l], w_up_diff[l], qg)
    z = _matmul(mixed, w_out[l], F32, tm=1024, tn=512, name="out_proj")
    x1, h2 = _resnorm(x2, z, attn_post_norm[l], mlp_pre_norm[l])

    u = _matmul(h2, w_mlp_in[l], BF16, tm=1024, tn=512, act="relu2", name="mlp_in")
    z2 = _matmul_ktiled(u, w_mlp_out[l], F32, tm=1024, tn=1024, tk=2048, name="mlp_out")
    out = _resnorm(x1, z2, mlp_post_norm[l])
    return out.reshape(bsz, seq, d)
```

```python
import functools
import math

import jax
import jax.numpy as jnp
from jax import lax
from jax.experimental import pallas as pl
from jax.experimental.pallas import tpu as pltpu

F32 = jnp.float32
BF16 = jnp.bfloat16

D_MODEL = 4096
D_RWKV = D_MODEL // 2
RWKV_HEAD = 64
RANK_LORA = 96
RANK_LORA_PAD = 128
RANK_GATE = 256
D_DIFF = D_MODEL // 2
DIFF_HEAD = 64
N_DIFF_HEADS = D_DIFF // (2 * DIFF_HEAD)
DIFF_VDIM = 2 * DIFF_HEAD
D_FF = 4 * D_MODEL
EPS_RMS = 1e-6
EPS_GN = 64e-5
EPS_SUBLN = 1e-5
LAMBDA_INIT = 0.8 - 0.6 * math.exp(-0.3 * 0)
LOG2E = math.log2(math.e)

HEAD_GROUP = 256
N_HEAD_GROUPS = D_RWKV // HEAD_GROUP
LORA_W = 2 * RANK_LORA_PAD + RANK_GATE
COL_GATE = 3 * D_DIFF
N_QG = COL_GATE + 2 * D_MODEL

CHUNK = 64
SCAN_GROUPS = 8
SCAN_CHUNKS_PER_STEP = 4
ATTN_SUM_ROWS = 16
ATTN_KEY_CHUNK = 512
ATTN_ROW_GROUP = 32
VMEM_LIMIT = 48 * 1024 * 1024


def _cparams(sem):
    return pltpu.CompilerParams(dimension_semantics=sem, vmem_limit_bytes=VMEM_LIMIT)


def _rms(t, g):
    return t * lax.rsqrt(jnp.mean(t * t, axis=-1, keepdims=True) + EPS_RMS) * g


def _prenorm_body(x_ref, g_ref, o_ref):
    o_ref[...] = _rms(x_ref[...], g_ref[...]).astype(o_ref.dtype)


def _prenorm(x, g, tr=256):
    m, d = x.shape
    return pl.pallas_call(
        _prenorm_body,
        grid=(m // tr,),
        in_specs=[pl.BlockSpec((tr, d), lambda i: (i, 0)), pl.BlockSpec((1, d), lambda i: (0, 0))],
        out_specs=pl.BlockSpec((tr, d), lambda i: (i, 0)),
        out_shape=jax.ShapeDtypeStruct((m, d), BF16),
        compiler_params=_cparams(("parallel",)),
        name="prenorm",
    )(x, g.reshape(1, d))


def _resnorm_body(x_ref, z_ref, g_ref, g2_ref, o_ref, h_ref):
    y = x_ref[...] + _rms(z_ref[...].astype(F32), g_ref[...])
    o_ref[...] = y
    if h_ref is not None:
        h_ref[...] = _rms(y, g2_ref[...]).astype(h_ref.dtype)


def _resnorm(x, z, g, g_next=None, tr=256):
    m, d = x.shape
    row = pl.BlockSpec((tr, d), lambda i: (i, 0))
    vec = pl.BlockSpec((1, d), lambda i: (0, 0))
    if g_next is None:
        body = lambda x_ref, z_ref, g_ref, o_ref: _resnorm_body(x_ref, z_ref, g_ref, None, o_ref, None)
        return pl.pallas_call(
            body, grid=(m // tr,), in_specs=[row, row, vec], out_specs=row,
            out_shape=jax.ShapeDtypeStruct((m, d), F32),
            compiler_params=_cparams(("parallel",)), name="resnorm_out",
        )(x, z, g.reshape(1, d))
    return pl.pallas_call(
        _resnorm_body, grid=(m // tr,), in_specs=[row, row, vec, vec], out_specs=[row, row],
        out_shape=[jax.ShapeDtypeStruct((m, d), F32), jax.ShapeDtypeStruct((m, d), BF16)],
        compiler_params=_cparams(("parallel",)), name="resnorm_mid",
    )(x, z, g.reshape(1, d), g_next.reshape(1, d))


def _mm_body(a_ref, b_ref, *rest, act, b_transposed, scaled):
    o_ref = rest[-1]
    dims = (((1,), (1 if b_transposed else 0,)), ((), ()))
    acc = lax.dot_general(a_ref[...], b_ref[...].astype(BF16), dims, preferred_element_type=F32)
    if act == "relu2":
        acc = jnp.square(jnp.maximum(acc, 0.0))
    if scaled:
        acc = acc * rest[0][...]
    o_ref[...] = acc.astype(o_ref.dtype)


def _matmul(a, b, out_dtype, tm, tn, act=None, col_scale=None, bt_rows=None, name="matmul"):
    m, k = a.shape
    b_transposed = bt_rows is not None
    ij = lambda g0, g1: (g0, g1)
    if b_transposed:
        first, n = bt_rows
        b_block = (tn, k)
        b_spec = pl.BlockSpec((pl.Element(tn), pl.Element(k)),
                              lambda g0, g1: (pl.multiple_of(first + ij(g0, g1)[1] * tn, 16), 0))
    else:
        n = b.shape[1]
        b_block = (k, tn)
        b_spec = pl.BlockSpec(b_block, lambda g0, g1: (0, ij(g0, g1)[1]))
    a_mode = pl.Buffered(1) if n // tn >= 4 else None
    in_specs = [pl.BlockSpec((tm, k), lambda g0, g1: (ij(g0, g1)[0], 0), pipeline_mode=a_mode), b_spec]
    args = (a, b)
    if col_scale is not None:
        in_specs.append(pl.BlockSpec((1, tn), lambda g0, g1: (0, ij(g0, g1)[1])))
        args = (a, b, col_scale)
    body = functools.partial(_mm_body, act=act, b_transposed=b_transposed, scaled=col_scale is not None)
    return pl.pallas_call(
        body,
        grid=(m // tm, n // tn),
        in_specs=in_specs,
        out_specs=pl.BlockSpec((tm, tn), lambda g0, g1: ij(g0, g1)),
        out_shape=jax.ShapeDtypeStruct((m, n), out_dtype),
        compiler_params=_cparams(("parallel", "arbitrary")),
        name=name,
    )(*args)


def _mmk_body(a_ref, b_ref, o_ref, acc_ref):
    kk = pl.program_id(2)

    @pl.when(kk == 0)
    def _():
        acc_ref[...] = jnp.zeros_like(acc_ref)

    acc_ref[...] += jnp.dot(a_ref[...], b_ref[...].astype(BF16), preferred_element_type=F32)

    @pl.when(kk == pl.num_programs(2) - 1)
    def _():
        o_ref[...] = acc_ref[...].astype(o_ref.dtype)


def _matmul_ktiled(a, b, out_dtype, tm, tn, tk, name="matmul_k"):
    m, k = a.shape
    _, n = b.shape
    return pl.pallas_call(
        _mmk_body,
        grid=(m // tm, n // tn, k // tk),
        in_specs=[pl.BlockSpec((tm, tk), lambda i, j, q: (i, q)), pl.BlockSpec((tk, tn), lambda i, j, q: (q, j))],
        out_specs=pl.BlockSpec((tm, tn), lambda i, j, q: (i, j)),
        out_shape=jax.ShapeDtypeStruct((m, n), out_dtype),
        scratch_shapes=[pltpu.VMEM((tm, tn), F32)],
        compiler_params=_cparams(("parallel", "parallel", "arbitrary")),
        name=name,
    )(a, b)


def _sigmoid(t):
    return 1.0 / (1.0 + jnp.exp(-t))


def _upgate_body(ya_ref, yb_ref, wa_ref, wb_ref, ga_ref, gb_ref, o_ref):
    acc_a = jnp.dot(ya_ref[...], wa_ref[...].astype(BF16), preferred_element_type=F32)
    acc_b = jnp.dot(yb_ref[...], wb_ref[...].astype(BF16), preferred_element_type=F32)
    ga = _sigmoid(ga_ref[...].astype(F32))
    gb = _sigmoid(gb_ref[...].astype(F32))
    o_ref[...] = (ga * acc_a + gb * acc_b).astype(o_ref.dtype)


def _upgate(ya, yb, wa, wb, qg, tm=1024, tn=512):
    m, k = ya.shape
    n = wa.shape[1]
    ga0 = COL_GATE // tn
    gb0 = (COL_GATE + D_MODEL) // tn
    return pl.pallas_call(
        _upgate_body,
        grid=(m // tm, n // tn),
        in_specs=[
            pl.BlockSpec((tm, k), lambda i, j: (i, 0)),
            pl.BlockSpec((tm, k), lambda i, j: (i, 0)),
            pl.BlockSpec((k, tn), lambda i, j: (0, j)),
            pl.BlockSpec((k, tn), lambda i, j: (0, j)),
            pl.BlockSpec((tm, tn), lambda i, j: (i, ga0 + j)),
            pl.BlockSpec((tm, tn), lambda i, j: (i, gb0 + j)),
        ],
        out_specs=pl.BlockSpec((tm, tn), lambda i, j: (i, j)),
        out_shape=jax.ShapeDtypeStruct((m, n), BF16),
        compiler_params=_cparams(("parallel", "arbitrary")),
        name="upgate",
    )(ya, yb, wa, wb, qg, qg)


def _head_mask(rows, cols):
    r = lax.broadcasted_iota(jnp.int32, (rows, cols), 0) // RWKV_HEAD
    c = lax.broadcasted_iota(jnp.int32, (rows, cols), 1) // RWKV_HEAD
    return r == c


def _split_bf16(t):
    hi = t.astype(BF16)
    lo = (t - hi.astype(F32)).astype(BF16)
    return hi, lo


def _head_sum(t, ones_bd):
    hi, lo = _split_bf16(t)
    return (jnp.dot(hi, ones_bd, preferred_element_type=F32)
            + jnp.dot(lo, ones_bd, preferred_element_type=F32))


def _prep_body(r_ref, k_ref, v_ref, lo_ref,
               rp_ref, kp_ref, vp_ref, lop_ref, rn_ref, kn_ref, vn_ref, lon_ref,
               mup_ref, mun_ref, lmup_ref, lmun_ref,
               w0f_ref, w0b_ref, a0f_ref, a0b_ref, kk_ref, ka_ref, rk_ref,
               w2f_ref, w2b_ref, a2f_ref, a2b_ref, g2_ref,
               ro_ref, vo_ref, kko_ref, kdf_ref, kdb_ref, bf_ref, bb_ref, lwf_ref, lwb_ref,
               g_ref, bonus_ref,
               tw_s, da_s, sg_s, *, tb_rows):
    tb = pl.program_id(1)
    hg = pl.program_id(2)
    first = tb == 0
    last = tb == pl.num_programs(1) - 1

    def shift_mix(x_ref, p_ref, n_ref, mup, mun):
        x = x_ref[0].astype(F32)
        width = x.shape[1]
        prev_row = jnp.where(first, 0.0, p_ref[0][7:8, :].astype(F32))
        next_row = jnp.where(last, 0.0, n_ref[0][0:1, :].astype(F32))
        row = lax.broadcasted_iota(jnp.int32, (tb_rows, width), 0)
        xp = jnp.where(row == 0, prev_row, pltpu.roll(x, 1, axis=0))
        xn = jnp.where(row == tb_rows - 1, next_row, pltpu.roll(x, tb_rows - 1, axis=0))
        return x + mup * (xp - x) + mun * (xn - x)

    @pl.when(hg == 0)
    def _():
        lo = shift_mix(lo_ref, lop_ref, lon_ref, lmup_ref[...], lmun_ref[...])
        tw_s[...] = jnp.tanh(lo[:, :RANK_LORA_PAD]).astype(BF16)
        da_s[...] = lo[:, RANK_LORA_PAD:2 * RANK_LORA_PAD].astype(BF16)
        sg_s[...] = _sigmoid(lo[:, 2 * RANK_LORA_PAD:]).astype(BF16)

    mu_p = mup_ref[...]
    mu_n = mun_ref[...]
    r = shift_mix(r_ref, rp_ref, rn_ref, mu_p[:, 0:HEAD_GROUP], mu_n[:, 0:HEAD_GROUP])
    k = shift_mix(k_ref, kp_ref, kn_ref, mu_p[:, HEAD_GROUP:2 * HEAD_GROUP], mu_n[:, HEAD_GROUP:2 * HEAD_GROUP])
    v = shift_mix(v_ref, vp_ref, vn_ref, mu_p[:, 2 * HEAD_GROUP:], mu_n[:, 2 * HEAD_GROUP:])

    ones_bd = jnp.where(_head_mask(HEAD_GROUP, HEAD_GROUP), 1.0, 0.0).astype(BF16)
    kk = k * kk_ref[...]
    kk = kk * lax.rsqrt(jnp.maximum(_head_sum(kk * kk, ones_bd), 1e-24))

    tw = tw_s[...]
    da = da_s[...]

    def direction(w0_ref, w2_ref, a0_ref, a2_ref):
        dec = w0_ref[...] + jnp.dot(tw, w2_ref[...], preferred_element_type=F32)
        z = -dec
        softplus = jnp.maximum(z, 0.0) + jnp.log1p(jnp.exp(-jnp.abs(z)))
        lw = -jnp.exp(-softplus - 0.5)
        a = _sigmoid(a0_ref[...] + jnp.dot(da, a2_ref[...], preferred_element_type=F32))
        kd = k * (1.0 + (a - 1.0) * ka_ref[...])
        return lw, a, kd

    lw_f, a_f, kd_f = direction(w0f_ref, w2f_ref, a0f_ref, a2f_ref)
    lw_b, a_b, kd_b = direction(w0b_ref, w2b_ref, a0b_ref, a2b_ref)

    bonus = _head_sum(r * (0.5 * (kd_f + kd_b)) * rk_ref[...], ones_bd) * v
    g = jnp.dot(sg_s[...], g2_ref[...], preferred_element_type=F32)

    ro_ref[0] = r.astype(ro_ref.dtype)
    vo_ref[0] = v.astype(vo_ref.dtype)
    kko_ref[0] = kk.astype(kko_ref.dtype)
    kdf_ref[0] = kd_f.astype(kdf_ref.dtype)
    kdb_ref[0] = kd_b.astype(kdb_ref.dtype)
    bf_ref[0] = (kk * a_f).astype(bf_ref.dtype)
    bb_ref[0] = (kk * a_b).astype(bb_ref.dtype)
    lwf_ref[0] = lw_f.astype(lwf_ref.dtype)
    lwb_ref[0] = lw_b.astype(lwb_ref.dtype)
    g_ref[0] = g.astype(g_ref.dtype)
    bonus_ref[0] = bonus.astype(bonus_ref.dtype)


def _rwkv_prep(rkv3, lora3, vecs, mats, tb_rows=2048):
    bsz, seq, _ = rkv3.shape
    tb_rows = min(tb_rows, seq)
    ntb = seq // tb_rows
    hgw = HEAD_GROUP
    n8 = seq // 8
    r0, k0, v0 = 0, D_RWKV // hgw, 2 * D_RWKV // hgw
    lo0 = 0

    def main(c0):
        return pl.BlockSpec((1, tb_rows, hgw), lambda b, t, h: (b, t, c0 + h))

    def prev(c0):
        return pl.BlockSpec((1, 8, hgw), lambda b, t, h: (b, jnp.maximum(t * (tb_rows // 8) - 1, 0), c0 + h))

    def nxt(c0):
        return pl.BlockSpec((1, 8, hgw), lambda b, t, h: (b, jnp.minimum((t + 1) * (tb_rows // 8), n8 - 1), c0 + h))

    lo_main = pl.BlockSpec((1, tb_rows, LORA_W), lambda b, t, h: (b, t, lo0))
    lo_prev = pl.BlockSpec((1, 8, LORA_W), lambda b, t, h: (b, jnp.maximum(t * (tb_rows // 8) - 1, 0), lo0))
    lo_next = pl.BlockSpec((1, 8, LORA_W), lambda b, t, h: (b, jnp.minimum((t + 1) * (tb_rows // 8), n8 - 1), lo0))

    def hvec():
        return pl.BlockSpec((1, hgw), lambda b, t, h: (0, h))

    def hmat(rows):
        return pl.BlockSpec((rows, hgw), lambda b, t, h: (0, h))

    mu3 = pl.BlockSpec((None, 1, 3 * hgw), lambda b, t, h: (h, 0, 0))
    full = lambda shape: pl.BlockSpec(shape, lambda b, t, h: (0,) * len(shape))

    in_specs = [main(r0), main(k0), main(v0), lo_main,
                prev(r0), prev(k0), prev(v0), lo_prev, nxt(r0), nxt(k0), nxt(v0), lo_next,
                mu3, mu3, full((1, LORA_W)), full((1, LORA_W)),
                hvec(), hvec(), hvec(), hvec(), hvec(), hvec(), hvec(),
                hmat(RANK_LORA_PAD), hmat(RANK_LORA_PAD), hmat(RANK_LORA_PAD), hmat(RANK_LORA_PAD),
                hmat(RANK_GATE)]
    out_block = pl.BlockSpec((1, tb_rows, hgw), lambda b, t, h: (b, t, h))
    shp = lambda dt: jax.ShapeDtypeStruct((bsz, seq, D_RWKV), dt)
    out_dtypes = [BF16] * 11
    return pl.pallas_call(
        functools.partial(_prep_body, tb_rows=tb_rows),
        grid=(bsz, ntb, N_HEAD_GROUPS),
        in_specs=in_specs,
        out_specs=[out_block] * len(out_dtypes),
        out_shape=[shp(dt) for dt in out_dtypes],
        scratch_shapes=[pltpu.VMEM((tb_rows, RANK_LORA_PAD), BF16),
                        pltpu.VMEM((tb_rows, RANK_LORA_PAD), BF16),
                        pltpu.VMEM((tb_rows, RANK_GATE), BF16)],
        compiler_params=_cparams(("parallel", "parallel", "arbitrary")),
        name="rwkv_prep",
    )(rkv3, rkv3, rkv3, lora3, rkv3, rkv3, rkv3, lora3, rkv3, rkv3, rkv3, lora3,
      vecs["mu_prev3"], vecs["mu_next3"], vecs["lmu_prev"], vecs["lmu_next"],
      vecs["w0f"], vecs["w0b"], vecs["a0f"], vecs["a0b"], vecs["k_k"], vecs["k_a"], vecs["r_k"],
      mats["w2f"], mats["w2b"], mats["a2f"], mats["a2b"], mats["g2"])


def _dot_nt(a, b):
    return lax.dot_general(a, b, (((1,), (1,)), ((), ())), preferred_element_type=F32)


def _block_diag(t, mask):
    return jnp.where(mask, jnp.concatenate([t, t, t, t], axis=0), jnp.zeros((), t.dtype))


def _scan_masks():
    c = CHUNK
    heads = HEAD_GROUP // RWKV_HEAD
    t_idx = lax.broadcasted_iota(jnp.int32, (c, HEAD_GROUP), 0)
    s_idx = lax.broadcasted_iota(jnp.int32, (c, HEAD_GROUP), 1) % c
    n_idx = lax.broadcasted_iota(jnp.int32, (RWKV_HEAD, HEAD_GROUP), 0)
    j_idx = lax.broadcasted_iota(jnp.int32, (RWKV_HEAD, HEAD_GROUP), 1)
    r2 = lax.broadcasted_iota(jnp.int32, (heads * 2 * c, HEAD_GROUP), 0) // (2 * c)
    c2 = lax.broadcasted_iota(jnp.int32, (heads * 2 * c, HEAD_GROUP), 1) // RWKV_HEAD
    return dict(
        bd=_head_mask(HEAD_GROUP, HEAD_GROUP),
        bd2=r2 == c2,
        before={False: s_idx < t_idx, True: s_idx > t_idx},
        upto={False: s_idx <= t_idx, True: s_idx >= t_idx},
        eye=jnp.where(s_idx == t_idx, 1.0, 0.0),
        diag=n_idx == j_idx % RWKV_HEAD)


def _scan_chunks(chains, mk):
    c = CHUNK
    bd_mask = mk["bd"]
    n = len(chains)
    rev = [ch["reverse"] for ch in chains]
    strict = [mk["before"][x] for x in rev]
    incl = [mk["upto"][x] for x in rev]
    bd = lambda t: _block_diag(t, bd_mask)
    mm = lambda a, b: jnp.dot(a, b, preferred_element_type=F32)
    each = lambda f, *cols: [f(*xs) for xs in zip(*cols)]

    tri = {x: jnp.where(mk["upto"][x][:, :c], 1.0, 0.0).astype(BF16) for x in set(rev)}
    lw = [ch["lw"] for ch in chains]
    cum = [mm(tri[x], t.astype(BF16)) for x, t in zip(rev, lw)]
    cum_edge = [t[0:1, :] if x else t[c - 1:c, :] for x, t in zip(rev, cum)]
    e_pos = each(jnp.exp, cum)
    e_neg = [jnp.exp(-t) for t in cum]
    g_edge = each(jnp.exp, cum_edge)
    e_edge = each(lambda g, e: g * e, g_edge, e_neg)
    a_t16 = [(-ch["kk"] * jnp.exp(t - l)).astype(BF16) for ch, t, l in zip(chains, cum, lw)]
    r_t = [ch["r"] * e for ch, e in zip(chains, e_pos)]
    b_t16 = [(ch["beta"] * e).astype(BF16) for ch, e in zip(chains, e_neg)]
    k_t16 = [(ch["kd"] * e).astype(BF16) for ch, e in zip(chains, e_neg)]
    b_p = [ch["beta"] * e for ch, e in zip(chains, e_edge)]
    k_p = [ch["kd"] * e for ch, e in zip(chains, e_edge)]
    v16 = [ch["v"].astype(BF16) for ch in chains]
    bd_v = each(bd, v16)

    ar = [jnp.concatenate([a, r.astype(BF16)], axis=0) for a, r in zip(a_t16, r_t)]
    pb = [_dot_nt(x, bd(b)) for x, b in zip(ar, b_t16)]
    pk = [_dot_nt(x, bd(k)) for x, k in zip(ar, k_t16)]
    p = [jnp.where(m, t[:c], 0.0) for m, t in zip(strict, pb)]
    q16 = [jnp.where(m, t[:c], 0.0).astype(BF16) for m, t in zip(strict, pk)]
    mrb16 = [jnp.where(m, t[c:], 0.0).astype(BF16) for m, t in zip(incl, pb)]
    mrk16 = [jnp.where(m, t[c:], 0.0).astype(BF16) for m, t in zip(incl, pk)]

    qv16 = [mm(a, b).astype(BF16) for a, b in zip(q16, bd_v)]

    tmat = [mk["eye"] + t for t in p]
    p16 = [t.astype(BF16) for t in p]
    pw = [mm(t, bd(t)) for t in p16]
    levels = int(math.log2(c))
    for lev in range(1, levels):
        bd_pw = [bd(t.astype(BF16)) for t in pw]
        if lev < levels - 1:
            both = [mm(jnp.concatenate([t, w], axis=0).astype(BF16), b) for t, w, b in zip(tmat, pw, bd_pw)]
            tmat = [t + x[:c] for t, x in zip(tmat, both)]
            pw = [x[c:] for x in both]
        else:
            tmat = [t + mm(t.astype(BF16), b) for t, b in zip(tmat, bd_pw)]

    t16 = [t.astype(BF16) for t in tmat]
    a16 = [mm(t, bd(a)).astype(BF16) for t, a in zip(t16, a_t16)]
    w16 = [mm(t, bd(x)).astype(BF16) for t, x in zip(t16, qv16)]

    def packed_t(b, k):
        zt = jnp.concatenate([b, k], axis=0).T
        hd = RWKV_HEAD
        return jnp.concatenate([zt[h * hd:(h + 1) * hd] for h in range(HEAD_GROUP // hd)], axis=1).astype(BF16)

    def bd2(y):
        return jnp.where(mk["bd2"], jnp.concatenate([y, y, y, y], axis=0), jnp.zeros((), y.dtype))

    zt16 = each(packed_t, b_p, k_p)
    rhs_g = [bd2(jnp.concatenate([a, jnp.zeros_like(a)], axis=0)) for a in a16]
    rhs_h = [bd2(jnp.concatenate([w, v], axis=0)) for w, v in zip(w16, v16)]
    gh = [mm(z, jnp.concatenate([g, h], axis=1)) for z, g, h in zip(zt16, rhs_g, rhs_h)]
    g_mat = [t[:, :HEAD_GROUP] + jnp.where(mk["diag"], g, 0.0) for t, g in zip(gh, g_edge)]
    h_mat = [t[:, HEAD_GROUP:] for t in gh]

    r_hat = [r + mm(m, bd(a)) for r, m, a in zip(r_t, mrb16, a16)]
    o_intra = [mm(m, bd(w)) + mm(mk_, bv) for m, w, mk_, bv in zip(mrb16, w16, mrk16, bd_v)]

    both = [mm(jnp.concatenate([r, g], axis=0).astype(BF16), bd(ch["state"].astype(BF16)))
            for r, g, ch in zip(r_hat, g_mat, chains)]
    outs = [x[:c] + o for x, o in zip(both, o_intra)]
    states = [x[c:] + h for x, h in zip(both, h_mat)]
    return outs, states


def _scan_body(rf_ref, vf_ref, kkf_ref, kdf_ref, bf_ref, lwf_ref,
               rb_ref, vb_ref, kkb_ref, kdb_ref, bb_ref, lwb_ref,
               of_ref, ob_ref, sf_ref, sb_ref, *, groups):
    @pl.when(pl.program_id(2) == 0)
    def _():
        sf_ref[...] = jnp.zeros_like(sf_ref)
        sb_ref[...] = jnp.zeros_like(sb_ref)

    mk = _scan_masks()
    n_sub = rf_ref.shape[1] // CHUNK

    def advance(i, carry):
        row_f = pl.multiple_of(i * CHUNK, CHUNK)
        row_b = pl.multiple_of((n_sub - 1 - i) * CHUNK, CHUNK)
        chains, sinks = [], []
        for gi in range(groups):
            cols = slice(gi * HEAD_GROUP, (gi + 1) * HEAD_GROUP)
            ld = lambda ref, row: ref[0, pl.ds(row, CHUNK), cols].astype(F32)
            chains.append(dict(r=ld(rf_ref, row_f), v=ld(vf_ref, row_f), kk=ld(kkf_ref, row_f),
                               kd=ld(kdf_ref, row_f), beta=ld(bf_ref, row_f), lw=ld(lwf_ref, row_f),
                               state=sf_ref[:, cols], reverse=False))
            sinks.append((of_ref, sf_ref, row_f, cols))
            chains.append(dict(r=ld(rb_ref, row_b), v=ld(vb_ref, row_b), kk=ld(kkb_ref, row_b),
                               kd=ld(kdb_ref, row_b), beta=ld(bb_ref, row_b), lw=ld(lwb_ref, row_b),
                               state=sb_ref[:, cols], reverse=True))
            sinks.append((ob_ref, sb_ref, row_b, cols))
        outs, states = _scan_chunks(chains, mk)
        for (o_ref, s_ref, row, cols), o, s in zip(sinks, outs, states):
            o_ref[0, pl.ds(row, CHUNK), cols] = o.astype(o_ref.dtype)
            s_ref[:, cols] = s
        return carry

    lax.fori_loop(0, n_sub, advance, 0)


def _rwkv_scan(r, v, kk, kd_f, kd_b, beta_f, beta_b, lw_f, lw_b, groups=SCAN_GROUPS, n_sub=SCAN_CHUNKS_PER_STEP):
    bsz, seq, _ = r.shape
    nc = seq // (CHUNK * n_sub)
    width = groups * HEAD_GROUP
    fwd = pl.BlockSpec((1, CHUNK * n_sub, width), lambda b, h, c: (b, c, h))
    bwd = pl.BlockSpec((1, CHUNK * n_sub, width), lambda b, h, c: (b, nc - 1 - c, h))
    out = jax.ShapeDtypeStruct((bsz, seq, D_RWKV), BF16)
    return pl.pallas_call(
        functools.partial(_scan_body, groups=groups),
        grid=(bsz, D_RWKV // width, nc),
        in_specs=[fwd] * 6 + [bwd] * 6,
        out_specs=[fwd, bwd],
        out_shape=[out, out],
        scratch_shapes=[pltpu.VMEM((RWKV_HEAD, width), F32), pltpu.VMEM((RWKV_HEAD, width), F32)],
        compiler_params=_cparams(("parallel", "parallel", "arbitrary")),
        name="rwkv_scan",
    )(r, v, kk, kd_f, beta_f, lw_f, r, v, kk, kd_b, beta_b, lw_b)


def _post_body(of_ref, ob_ref, bonus_ref, g_ref, lng_ref, lnb_ref, y_ref):
    ones_bd = jnp.where(_head_mask(HEAD_GROUP, HEAD_GROUP), 1.0, 0.0).astype(BF16)
    o = of_ref[0].astype(F32) + ob_ref[0].astype(F32)
    inv_n = 1.0 / RWKV_HEAD
    mu = _head_sum(o, ones_bd) * inv_n
    d = o - mu
    var = _head_sum(d * d, ones_bd) * inv_n
    on = d * lax.rsqrt(var + EPS_GN) * lng_ref[...] + lnb_ref[...]
    y_ref[0] = ((on + bonus_ref[0].astype(F32)) * g_ref[0].astype(F32)).astype(y_ref.dtype)


def _rwkv_post(o_f, o_b, bonus, g, ln_g, ln_b, tb_rows=4096):
    bsz, seq, _ = o_f.shape
    tb_rows = min(tb_rows, seq)
    blk = pl.BlockSpec((1, tb_rows, HEAD_GROUP), lambda b, t, h: (b, t, h))
    vec = pl.BlockSpec((1, HEAD_GROUP), lambda b, t, h: (0, h))
    return pl.pallas_call(
        _post_body,
        grid=(bsz, seq // tb_rows, N_HEAD_GROUPS),
        in_specs=[blk, blk, blk, blk, vec, vec],
        out_specs=blk,
        out_shape=jax.ShapeDtypeStruct((bsz, seq, D_RWKV), BF16),
        compiler_params=_cparams(("parallel", "parallel", "parallel")),
        name="rwkv_post",
    )(o_f, o_b, bonus, g, ln_g, ln_b)


def _attn_body(qc_ref, qn_ref, k_ref, v_ref, lq1_ref, lk1_ref, lq2_ref, lk2_ref, sg_ref, o_ref,
               tbl_ref, vt_ref, t_ref, e_ref, *, tq):
    head = pl.program_id(1)
    qb = pl.program_id(2)
    nq = pl.num_programs(2)
    seq = k_ref.shape[1]
    d = DIFF_HEAD
    n_chunk = seq // ATTN_KEY_CHUNK
    n_quarter = 4
    rows_q = seq // n_quarter

    def stacked(q):
        lane = lax.broadcasted_iota(jnp.int32, (tq, 2 * d), 1)
        zero = jnp.zeros((), BF16)
        return jnp.concatenate([jnp.where(lane < d, q, zero), jnp.where(lane >= d, q, zero)], axis=0)

    def score_chunk(slot, qq, blk, c):
        rows = slice(c * ATTN_KEY_CHUNK, (c + 1) * ATTN_KEY_CHUNK)
        start = seq - tq - blk * tq + c * ATTN_KEY_CHUNK
        bias = tbl_ref[pl.ds(pl.multiple_of(start, 8), ATTN_KEY_CHUNK), :]
        t_ref[slot, rows, :] = _dot_nt(k_ref[0, rows, :], qq) + jnp.concatenate([bias, bias], axis=1)

    @pl.when(qb == 0)
    def _():
        slope = jnp.exp2(jnp.full((1, 1), -8.0 / N_DIFF_HEADS, F32) * (head + 1).astype(F32)) * LOG2E
        row = lax.broadcasted_iota(jnp.int32, tbl_ref.shape, 0)
        col = lax.broadcasted_iota(jnp.int32, tbl_ref.shape, 1)
        tbl_ref[...] = -slope * jnp.abs(col - row + (seq - tq)).astype(F32)
        vt_ref[0:2 * d, :] = v_ref[0].astype(F32).T.astype(BF16)
        ones_row = lax.broadcasted_iota(jnp.int32, (ATTN_SUM_ROWS, seq), 0) == 0
        vt_ref[2 * d:, :] = jnp.where(ones_row, 1.0, 0.0).astype(BF16)
        qq0 = stacked(qc_ref[0])
        for c in range(n_chunk):
            score_chunk(0, qq0, 0, c)

    lam = (jnp.exp(jnp.sum(lq1_ref[...] * lk1_ref[...], axis=-1, keepdims=True))
           - jnp.exp(jnp.sum(lq2_ref[...] * lk2_ref[...], axis=-1, keepdims=True)) + LAMBDA_INIT)

    blk_next = jnp.minimum(qb + 1, nq - 1)

    def step(cur, nxt):
        qq_next = stacked(qn_ref[0])
        next_chunks = iter(range(n_chunk))

        def issue_scores(count):
            for _ in range(count):
                c = next(next_chunks, None)
                if c is not None:
                    score_chunk(nxt, qq_next, blk_next, c)

        m_part = None
        for qtr in range(n_quarter):
            issue_scores(n_chunk // (2 * n_quarter))
            for g in range(rows_q // ATTN_ROW_GROUP):
                r0 = qtr * rows_q + g * ATTN_ROW_GROUP
                x = t_ref[cur, r0:r0 + ATTN_ROW_GROUP, :]
                for r in range(ATTN_ROW_GROUP // 8):
                    tile = x[r * 8:(r + 1) * 8, :]
                    m_part = tile if m_part is None else jnp.maximum(m_part, tile)
        m = jnp.max(m_part, axis=0, keepdims=True)

        aug = None
        for qtr in range(n_quarter):
            issue_scores(n_chunk // (2 * n_quarter))
            for g in range(rows_q // ATTN_ROW_GROUP):
                r0 = qtr * rows_q + g * ATTN_ROW_GROUP
                e_ref[r0:r0 + ATTN_ROW_GROUP, :] = jnp.exp2(
                    t_ref[cur, r0:r0 + ATTN_ROW_GROUP, :] - m).astype(BF16)
            rows = slice(qtr * rows_q, (qtr + 1) * rows_q)
            part = jnp.dot(vt_ref[:, rows], e_ref[rows, :], preferred_element_type=F32)
            aug = part if aug is None else aug + part
        issue_scores(n_chunk)

        acc = aug[:2 * d]
        l = aug[2 * d:2 * d + 1]
        out_t = acc[:, :tq] * (1.0 / l[:, :tq]) - acc[:, tq:] * (lam / l[:, tq:])
        out = out_t.T
        out = out * lax.rsqrt(jnp.mean(out * out, axis=-1, keepdims=True) + EPS_SUBLN) * sg_ref[...]
        o_ref[0] = (out * (1.0 - LAMBDA_INIT)).astype(o_ref.dtype)

    @pl.when(qb % 2 == 0)
    def _():
        step(0, 1)

    @pl.when(qb % 2 == 1)
    def _():
        step(1, 0)


def _diff_attn(qg3, lq1, lk1, lq2, lk2, subln_g, tq=256):
    bsz, seq, _ = qg3.shape
    w = DIFF_VDIM
    nq = seq // tq
    q0 = 0
    k0 = D_DIFF // w
    v0 = 2 * D_DIFF // w
    vec = lambda n: pl.BlockSpec((1, n), lambda b, h, i: (0, 0))
    return pl.pallas_call(
        functools.partial(_attn_body, tq=tq),
        grid=(bsz, N_DIFF_HEADS, nq),
        in_specs=[pl.BlockSpec((1, tq, w), lambda b, h, i: (b, i, q0 + h)),
                  pl.BlockSpec((1, tq, w), lambda b, h, i: (b, jnp.minimum(i + 1, nq - 1), q0 + h)),
                  pl.BlockSpec((1, seq, w), lambda b, h, i: (b, 0, k0 + h)),
                  pl.BlockSpec((1, seq, w), lambda b, h, i: (b, 0, v0 + h)),
                  vec(DIFF_HEAD), vec(DIFF_HEAD), vec(DIFF_HEAD), vec(DIFF_HEAD), vec(w)],
        out_specs=pl.BlockSpec((1, tq, w), lambda b, h, i: (b, i, h)),
        out_shape=jax.ShapeDtypeStruct((bsz, seq, D_DIFF), BF16),
        scratch_shapes=[pltpu.VMEM((2 * seq - tq, tq), F32),
                        pltpu.VMEM((w + ATTN_SUM_ROWS, seq), BF16),
                        pltpu.VMEM((2, seq, 2 * tq), F32),
                        pltpu.VMEM((seq, 2 * tq), BF16)],
        compiler_params=_cparams(("parallel", "parallel", "arbitrary")),
        name="diff_attn",
    )(qg3, qg3, qg3, qg3, lq1.reshape(1, -1), lk1.reshape(1, -1), lq2.reshape(1, -1), lk2.reshape(1, -1),
      subln_g.reshape(1, -1))


def _pad_cols(t, width):
    return jnp.pad(t, ((0, 0), (0, width - t.shape[1])))


def _pad_rows(t, rows):
    return jnp.pad(t, ((0, rows - t.shape[0]), (0, 0)))


def kernel(x, attn_pre_norm, attn_post_norm, w_in, shift_prev, shift_next, decay_bias_fwd, decay_up_fwd, decay_bias_bwd, decay_up_bwd, iclr_bias_fwd, iclr_up_fwd, iclr_bias_bwd, iclr_up_bwd, gate_up, k_k, k_a, r_k, ln_x_gain, ln_x_bias, lambda_q1, lambda_k1, lambda_q2, lambda_k2, subln_gain, w_up_rwkv, w_up_diff, w_out, mlp_pre_norm, mlp_post_norm, w_mlp_in, w_mlp_out):
    bsz, seq, d = x.shape
    m = bsz * seq
    l = 0
    x2 = x.reshape(m, d)

    wt = jnp.transpose(w_in[l])
    c_dw = 3 * D_RWKV
    c_da = c_dw + RANK_LORA
    c_dg = c_da + RANK_LORA
    c_q = c_dg + RANK_GATE
    wt_lora = jnp.concatenate([_pad_rows(wt[c_dw:c_da], RANK_LORA_PAD), _pad_rows(wt[c_da:c_dg], RANK_LORA_PAD),
                               wt[c_dg:c_q]], axis=0)

    def regroup(vec):
        t = vec.reshape(3, N_HEAD_GROUPS, HEAD_GROUP)
        return jnp.transpose(t, (1, 0, 2)).reshape(N_HEAD_GROUPS, 1, 3 * HEAD_GROUP)

    def lora_vec(vec):
        t = vec.reshape(1, -1)
        return jnp.concatenate([_pad_cols(t[:, c_dw:c_da], RANK_LORA_PAD), _pad_cols(t[:, c_da:c_dg], RANK_LORA_PAD),
                                t[:, c_dg:c_q]], axis=1)

    row = lambda t: t.reshape(1, -1)
    vecs = dict(
        mu_prev3=regroup(shift_prev[l][:c_dw]), mu_next3=regroup(shift_next[l][:c_dw]),
        lmu_prev=lora_vec(shift_prev[l]), lmu_next=lora_vec(shift_next[l]),
        w0f=row(decay_bias_fwd[l]), w0b=row(decay_bias_bwd[l]), a0f=row(iclr_bias_fwd[l]), a0b=row(iclr_bias_bwd[l]),
        k_k=row(k_k[l]), k_a=row(k_a[l]), r_k=row(r_k[l]))
    mats = dict(
        w2f=_pad_rows(decay_up_fwd[l], RANK_LORA_PAD).astype(BF16), w2b=_pad_rows(decay_up_bwd[l], RANK_LORA_PAD).astype(BF16),
        a2f=_pad_rows(iclr_up_fwd[l], RANK_LORA_PAD).astype(BF16), a2b=_pad_rows(iclr_up_bwd[l], RANK_LORA_PAD).astype(BF16),
        g2=gate_up[l].astype(BF16))

    h = _prenorm(x2, attn_pre_norm[l])
    col_scale = jnp.ones((1, N_QG), F32).at[:, :D_DIFF].set(DIFF_HEAD ** -0.5 * LOG2E)
    rkv3 = _matmul(h, wt, BF16, tm=2048, tn=512, bt_rows=(0, c_dw),
                   name="in_proj_rkv").reshape(bsz, seq, 3 * D_RWKV)
    lora3 = _matmul(h, wt_lora, BF16, tm=1024, tn=LORA_W, bt_rows=(0, LORA_W),
                    name="in_proj_lora").reshape(bsz, seq, LORA_W)
    qg = _matmul(h, wt, BF16, tm=2048, tn=512, col_scale=col_scale, bt_rows=(c_q, N_QG), name="in_proj_qg")

    r, v, kk, kd_f, kd_b, beta_f, beta_b, lw_f, lw_b, g, bonus = _rwkv_prep(rkv3, lora3, vecs, mats)
    o_f, o_b = _rwkv_scan(r, v, kk, kd_f, kd_b, beta_f, beta_b, lw_f, lw_b)
    y_a = _rwkv_post(o_f, o_b, bonus, g, row(ln_x_gain[l]), row(ln_x_bias[l]))

    y_b = _diff_attn(qg.reshape(bsz, seq, N_QG), lambda_q1[l], lambda_k1[l], lambda_q2[l], lambda_k2[l],
                     subln_gain[l])

    mixed = _upgate(y_a.reshape(m, D_RWKV), y_b.reshape(m, D_DIFF),
                    w_up_rwkv[l], w_up_diff[l], qg)
    z = _matmul(mixed, w_out[l], BF16, tm=2048, tn=512, name="out_proj")
    x1, h2 = _resnorm(x2, z, attn_post_norm[l], mlp_pre_norm[l])

    u = _matmul(h2, w_mlp_in[l], BF16, tm=2048, tn=512, act="relu2", name="mlp_in")
    z2 = _matmul_ktiled(u, w_mlp_out[l], BF16, tm=1024, tn=1024, tk=2048, name="mlp_out")
    out = _resnorm(x1, z2, mlp_post_norm[l])
    return out.reshape(bsz, seq, d)
```

```python
import functools
import math

import jax
import jax.numpy as jnp
from jax import lax
from jax.experimental import pallas as pl
from jax.experimental.pallas import tpu as pltpu

F32 = jnp.float32
BF16 = jnp.bfloat16

D_MODEL = 4096
D_RWKV = D_MODEL // 2
RWKV_HEAD = 64
RANK_LORA = 96
RANK_LORA_PAD = 128
RANK_GATE = 256
D_DIFF = D_MODEL // 2
DIFF_HEAD = 64
N_DIFF_HEADS = D_DIFF // (2 * DIFF_HEAD)
DIFF_VDIM = 2 * DIFF_HEAD
D_FF = 4 * D_MODEL
EPS_RMS = 1e-6
EPS_GN = 64e-5
EPS_SUBLN = 1e-5
LAMBDA_INIT = 0.8 - 0.6 * math.exp(-0.3 * 0)
LOG2E = math.log2(math.e)

HEAD_GROUP = 256
N_HEAD_GROUPS = D_RWKV // HEAD_GROUP
LORA_W = 2 * RANK_LORA_PAD + RANK_GATE
COL_GATE = 3 * D_DIFF
N_QG = COL_GATE + 2 * D_MODEL

CHUNK = 64
SCAN_GROUPS = 8
SCAN_CHUNKS_PER_STEP = 4
ATTN_SUM_ROWS = 16
ATTN_KEY_CHUNK = 512
ATTN_ROW_GROUP = 32
VMEM_LIMIT = 48 * 1024 * 1024


def _cparams(sem):
    return pltpu.CompilerParams(dimension_semantics=sem, vmem_limit_bytes=VMEM_LIMIT)


def _rms(t, g):
    return t * lax.rsqrt(jnp.mean(t * t, axis=-1, keepdims=True) + EPS_RMS) * g


def _prenorm_body(x_ref, g_ref, o_ref):
    o_ref[...] = _rms(x_ref[...], g_ref[...]).astype(o_ref.dtype)


def _prenorm(x, g, tr=256):
    m, d = x.shape
    return pl.pallas_call(
        _prenorm_body,
        grid=(m // tr,),
        in_specs=[pl.BlockSpec((tr, d), lambda i: (i, 0)), pl.BlockSpec((1, d), lambda i: (0, 0))],
        out_specs=pl.BlockSpec((tr, d), lambda i: (i, 0)),
        out_shape=jax.ShapeDtypeStruct((m, d), BF16),
        compiler_params=_cparams(("parallel",)),
        name="prenorm",
    )(x, g.reshape(1, d))


def _resnorm_body(x_ref, z_ref, g_ref, g2_ref, o_ref, h_ref):
    y = x_ref[...] + _rms(z_ref[...].astype(F32), g_ref[...])
    o_ref[...] = y
    if h_ref is not None:
        h_ref[...] = _rms(y, g2_ref[...]).astype(h_ref.dtype)


def _resnorm(x, z, g, g_next=None, tr=256):
    m, d = x.shape
    row = pl.BlockSpec((tr, d), lambda i: (i, 0))
    vec = pl.BlockSpec((1, d), lambda i: (0, 0))
    if g_next is None:
        body = lambda x_ref, z_ref, g_ref, o_ref: _resnorm_body(x_ref, z_ref, g_ref, None, o_ref, None)
        return pl.pallas_call(
            body, grid=(m // tr,), in_specs=[row, row, vec], out_specs=row,
            out_shape=jax.ShapeDtypeStruct((m, d), F32),
            compiler_params=_cparams(("parallel",)), name="resnorm_out",
        )(x, z, g.reshape(1, d))
    return pl.pallas_call(
        _resnorm_body, grid=(m // tr,), in_specs=[row, row, vec, vec], out_specs=[row, row],
        out_shape=[jax.ShapeDtypeStruct((m, d), F32), jax.ShapeDtypeStruct((m, d), BF16)],
        compiler_params=_cparams(("parallel",)), name="resnorm_mid",
    )(x, z, g.reshape(1, d), g_next.reshape(1, d))


def _mm_body(a_ref, b_ref, *rest, act, b_transposed, scaled):
    o_ref = rest[-1]
    dims = (((1,), (1 if b_transposed else 0,)), ((), ()))
    acc = lax.dot_general(a_ref[...], b_ref[...].astype(BF16), dims, preferred_element_type=F32)
    if act == "relu2":
        acc = jnp.square(jnp.maximum(acc, 0.0))
    if scaled:
        acc = acc * rest[0][...]
    o_ref[...] = acc.astype(o_ref.dtype)


def _matmul(a, b, out_dtype, tm, tn, act=None, col_scale=None, bt_rows=None, name="matmul"):
    m, k = a.shape
    b_transposed = bt_rows is not None
    ij = lambda g0, g1: (g0, g1)
    if b_transposed:
        first, n = bt_rows
        b_block = (tn, k)
        b_spec = pl.BlockSpec((pl.Element(tn), pl.Element(k)),
                              lambda g0, g1: (pl.multiple_of(first + ij(g0, g1)[1] * tn, 16), 0))
    else:
        n = b.shape[1]
        b_block = (k, tn)
        b_spec = pl.BlockSpec(b_block, lambda g0, g1: (0, ij(g0, g1)[1]))
    a_mode = pl.Buffered(1) if n // tn >= 4 else None
    in_specs = [pl.BlockSpec((tm, k), lambda g0, g1: (ij(g0, g1)[0], 0), pipeline_mode=a_mode), b_spec]
    args = (a, b)
    if col_scale is not None:
        in_specs.append(pl.BlockSpec((1, tn), lambda g0, g1: (0, ij(g0, g1)[1])))
        args = (a, b, col_scale)
    body = functools.partial(_mm_body, act=act, b_transposed=b_transposed, scaled=col_scale is not None)
    return pl.pallas_call(
        body,
        grid=(m // tm, n // tn),
        in_specs=in_specs,
        out_specs=pl.BlockSpec((tm, tn), lambda g0, g1: ij(g0, g1)),
        out_shape=jax.ShapeDtypeStruct((m, n), out_dtype),
        compiler_params=_cparams(("parallel", "arbitrary")),
        name=name,
    )(*args)


def _mmk_body(a_ref, b_ref, o_ref, acc_ref):
    kk = pl.program_id(2)

    @pl.when(kk == 0)
    def _():
        acc_ref[...] = jnp.zeros_like(acc_ref)

    acc_ref[...] += jnp.dot(a_ref[...], b_ref[...].astype(BF16), preferred_element_type=F32)

    @pl.when(kk == pl.num_programs(2) - 1)
    def _():
        o_ref[...] = acc_ref[...].astype(o_ref.dtype)


def _matmul_ktiled(a, b, out_dtype, tm, tn, tk, name="matmul_k"):
    m, k = a.shape
    _, n = b.shape
    return pl.pallas_call(
        _mmk_body,
        grid=(m // tm, n // tn, k // tk),
        in_specs=[pl.BlockSpec((tm, tk), lambda i, j, q: (i, q)), pl.BlockSpec((tk, tn), lambda i, j, q: (q, j))],
        out_specs=pl.BlockSpec((tm, tn), lambda i, j, q: (i, j)),
        out_shape=jax.ShapeDtypeStruct((m, n), out_dtype),
        scratch_shapes=[pltpu.VMEM((tm, tn), F32)],
        compiler_params=_cparams(("parallel", "parallel", "arbitrary")),
        name=name,
    )(a, b)


def _sigmoid(t):
    return 0.5 * jnp.tanh(0.5 * t) + 0.5


def _upgate_body(ya_ref, yb_ref, wa_ref, wb_ref, ga_ref, gb_ref, o_ref):
    acc_a = jnp.dot(ya_ref[...], wa_ref[...].astype(BF16), preferred_element_type=F32)
    acc_b = jnp.dot(yb_ref[...], wb_ref[...].astype(BF16), preferred_element_type=F32)
    ga = _sigmoid(ga_ref[...].astype(F32))
    gb = _sigmoid(gb_ref[...].astype(F32))
    o_ref[...] = (ga * acc_a + gb * acc_b).astype(o_ref.dtype)


def _upgate(ya, yb, wa, wb, qg, tm=1024, tn=512):
    m, k = ya.shape
    n = wa.shape[1]
    ga0 = COL_GATE // tn
    gb0 = (COL_GATE + D_MODEL) // tn
    return pl.pallas_call(
        _upgate_body,
        grid=(m // tm, n // tn),
        in_specs=[
            pl.BlockSpec((tm, k), lambda i, j: (i, 0)),
            pl.BlockSpec((tm, k), lambda i, j: (i, 0)),
            pl.BlockSpec((k, tn), lambda i, j: (0, j)),
            pl.BlockSpec((k, tn), lambda i, j: (0, j)),
            pl.BlockSpec((tm, tn), lambda i, j: (i, ga0 + j)),
            pl.BlockSpec((tm, tn), lambda i, j: (i, gb0 + j)),
        ],
        out_specs=pl.BlockSpec((tm, tn), lambda i, j: (i, j)),
        out_shape=jax.ShapeDtypeStruct((m, n), BF16),
        compiler_params=_cparams(("parallel", "arbitrary")),
        name="upgate",
    )(ya, yb, wa, wb, qg, qg)


def _head_mask(rows, cols):
    r = lax.broadcasted_iota(jnp.int32, (rows, cols), 0) // RWKV_HEAD
    c = lax.broadcasted_iota(jnp.int32, (rows, cols), 1) // RWKV_HEAD
    return r == c


def _split_bf16(t):
    hi = t.astype(BF16)
    lo = (t - hi.astype(F32)).astype(BF16)
    return hi, lo


def _head_sum(t, ones_bd):
    hi, lo = _split_bf16(t)
    return (jnp.dot(hi, ones_bd, preferred_element_type=F32)
            + jnp.dot(lo, ones_bd, preferred_element_type=F32))


def _prep_body(r_ref, k_ref, v_ref, lo_ref,
               rp_ref, kp_ref, vp_ref, lop_ref, rn_ref, kn_ref, vn_ref, lon_ref,
               mup_ref, mun_ref, lmup_ref, lmun_ref,
               w0f_ref, w0b_ref, a0f_ref, a0b_ref, kk_ref, ka_ref, rk_ref,
               w2f_ref, w2b_ref, a2f_ref, a2b_ref, g2_ref,
               ro_ref, vo_ref, kko_ref, kdf_ref, kdb_ref, bf_ref, bb_ref, lwf_ref, lwb_ref,
               g_ref, bonus_ref,
               tw_s, da_s, sg_s, *, tb_rows):
    tb = pl.program_id(1)
    hg = pl.program_id(2)
    first = tb == 0
    last = tb == pl.num_programs(1) - 1

    def shift_mix(x_ref, p_ref, n_ref, mup, mun):
        x = x_ref[0].astype(F32)
        width = x.shape[1]
        prev_row = jnp.where(first, 0.0, p_ref[0][7:8, :].astype(F32))
        next_row = jnp.where(last, 0.0, n_ref[0][0:1, :].astype(F32))
        row = lax.broadcasted_iota(jnp.int32, (tb_rows, width), 0)
        xp = jnp.where(row == 0, prev_row, pltpu.roll(x, 1, axis=0))
        xn = jnp.where(row == tb_rows - 1, next_row, pltpu.roll(x, tb_rows - 1, axis=0))
        return (1.0 - mup - mun) * x + mup * xp + mun * xn

    @pl.when(hg == 0)
    def _():
        lo = shift_mix(lo_ref, lop_ref, lon_ref, lmup_ref[...], lmun_ref[...])
        tw_s[...] = jnp.tanh(lo[:, :RANK_LORA_PAD]).astype(BF16)
        da_s[...] = lo[:, RANK_LORA_PAD:2 * RANK_LORA_PAD].astype(BF16)
        sg_s[...] = _sigmoid(lo[:, 2 * RANK_LORA_PAD:]).astype(BF16)

    mu_p = mup_ref[...]
    mu_n = mun_ref[...]
    r = shift_mix(r_ref, rp_ref, rn_ref, mu_p[:, 0:HEAD_GROUP], mu_n[:, 0:HEAD_GROUP])
    k = shift_mix(k_ref, kp_ref, kn_ref, mu_p[:, HEAD_GROUP:2 * HEAD_GROUP], mu_n[:, HEAD_GROUP:2 * HEAD_GROUP])
    v = shift_mix(v_ref, vp_ref, vn_ref, mu_p[:, 2 * HEAD_GROUP:], mu_n[:, 2 * HEAD_GROUP:])

    ones_bd = jnp.where(_head_mask(HEAD_GROUP, HEAD_GROUP), 1.0, 0.0).astype(BF16)
    kk = k * kk_ref[...]
    kk = kk * lax.rsqrt(jnp.maximum(_head_sum(kk * kk, ones_bd), 1e-24))

    tw = tw_s[...]
    da = da_s[...]

    def direction(w0_ref, w2_ref, a0_ref, a2_ref):
        dec = w0_ref[...] + jnp.dot(tw, w2_ref[...], preferred_element_type=F32)
        z = -dec
        softplus = jnp.maximum(z, 0.0) + jnp.log(1.0 + jnp.exp(-jnp.abs(z)))
        lw = -jnp.exp(-softplus - 0.5)
        a = _sigmoid(a0_ref[...] + jnp.dot(da, a2_ref[...], preferred_element_type=F32))
        kd = k * (1.0 + (a - 1.0) * ka_ref[...])
        return lw, a, kd

    lw_f, a_f, kd_f = direction(w0f_ref, w2f_ref, a0f_ref, a2f_ref)
    lw_b, a_b, kd_b = direction(w0b_ref, w2b_ref, a0b_ref, a2b_ref)

    bonus = _head_sum(r * (0.5 * (kd_f + kd_b)) * rk_ref[...], ones_bd) * v
    g = jnp.dot(sg_s[...], g2_ref[...], preferred_element_type=F32)

    ro_ref[0] = r.astype(ro_ref.dtype)
    vo_ref[0] = v.astype(vo_ref.dtype)
    kko_ref[0] = kk.astype(kko_ref.dtype)
    kdf_ref[0] = kd_f.astype(kdf_ref.dtype)
    kdb_ref[0] = kd_b.astype(kdb_ref.dtype)
    bf_ref[0] = (kk * a_f).astype(bf_ref.dtype)
    bb_ref[0] = (kk * a_b).astype(bb_ref.dtype)
    lwf_ref[0] = lw_f.astype(lwf_ref.dtype)
    lwb_ref[0] = lw_b.astype(lwb_ref.dtype)
    g_ref[0] = g.astype(g_ref.dtype)
    bonus_ref[0] = bonus.astype(bonus_ref.dtype)


def _rwkv_prep(rkv3, lora3, vecs, mats, tb_rows=2048):
    bsz, seq, _ = rkv3.shape
    tb_rows = min(tb_rows, seq)
    ntb = seq // tb_rows
    hgw = HEAD_GROUP
    n8 = seq // 8
    r0, k0, v0 = 0, D_RWKV // hgw, 2 * D_RWKV // hgw
    lo0 = 0

    def main(c0):
        return pl.BlockSpec((1, tb_rows, hgw), lambda b, t, h: (b, t, c0 + h))

    def prev(c0):
        return pl.BlockSpec((1, 8, hgw), lambda b, t, h: (b, jnp.maximum(t * (tb_rows // 8) - 1, 0), c0 + h))

    def nxt(c0):
        return pl.BlockSpec((1, 8, hgw), lambda b, t, h: (b, jnp.minimum((t + 1) * (tb_rows // 8), n8 - 1), c0 + h))

    lo_main = pl.BlockSpec((1, tb_rows, LORA_W), lambda b, t, h: (b, t, lo0))
    lo_prev = pl.BlockSpec((1, 8, LORA_W), lambda b, t, h: (b, jnp.maximum(t * (tb_rows // 8) - 1, 0), lo0))
    lo_next = pl.BlockSpec((1, 8, LORA_W), lambda b, t, h: (b, jnp.minimum((t + 1) * (tb_rows // 8), n8 - 1), lo0))

    def hvec():
        return pl.BlockSpec((1, hgw), lambda b, t, h: (0, h))

    def hmat(rows):
        return pl.BlockSpec((rows, hgw), lambda b, t, h: (0, h))

    mu3 = pl.BlockSpec((None, 1, 3 * hgw), lambda b, t, h: (h, 0, 0))
    full = lambda shape: pl.BlockSpec(shape, lambda b, t, h: (0,) * len(shape))

    in_specs = [main(r0), main(k0), main(v0), lo_main,
                prev(r0), prev(k0), prev(v0), lo_prev, nxt(r0), nxt(k0), nxt(v0), lo_next,
                mu3, mu3, full((1, LORA_W)), full((1, LORA_W)),
                hvec(), hvec(), hvec(), hvec(), hvec(), hvec(), hvec(),
                hmat(RANK_LORA_PAD), hmat(RANK_LORA_PAD), hmat(RANK_LORA_PAD), hmat(RANK_LORA_PAD),
                hmat(RANK_GATE)]
    out_block = pl.BlockSpec((1, tb_rows, hgw), lambda b, t, h: (b, t, h))
    shp = lambda dt: jax.ShapeDtypeStruct((bsz, seq, D_RWKV), dt)
    out_dtypes = [BF16] * 11
    return pl.pallas_call(
        functools.partial(_prep_body, tb_rows=tb_rows),
        grid=(bsz, ntb, N_HEAD_GROUPS),
        in_specs=in_specs,
        out_specs=[out_block] * len(out_dtypes),
        out_shape=[shp(dt) for dt in out_dtypes],
        scratch_shapes=[pltpu.VMEM((tb_rows, RANK_LORA_PAD), BF16),
                        pltpu.VMEM((tb_rows, RANK_LORA_PAD), BF16),
                        pltpu.VMEM((tb_rows, RANK_GATE), BF16)],
        compiler_params=_cparams(("parallel", "parallel", "arbitrary")),
        name="rwkv_prep",
    )(rkv3, rkv3, rkv3, lora3, rkv3, rkv3, rkv3, lora3, rkv3, rkv3, rkv3, lora3,
      vecs["mu_prev3"], vecs["mu_next3"], vecs["lmu_prev"], vecs["lmu_next"],
      vecs["w0f"], vecs["w0b"], vecs["a0f"], vecs["a0b"], vecs["k_k"], vecs["k_a"], vecs["r_k"],
      mats["w2f"], mats["w2b"], mats["a2f"], mats["a2b"], mats["g2"])


def _dot_nt(a, b):
    return lax.dot_general(a, b, (((1,), (1,)), ((), ())), preferred_element_type=F32)


def _block_diag(t, mask):
    return jnp.where(mask, jnp.concatenate([t, t, t, t], axis=0), jnp.zeros((), t.dtype))


def _scan_masks():
    c = CHUNK
    heads = HEAD_GROUP // RWKV_HEAD
    t_idx = lax.broadcasted_iota(jnp.int32, (c, HEAD_GROUP), 0)
    s_idx = lax.broadcasted_iota(jnp.int32, (c, HEAD_GROUP), 1) % c
    n_idx = lax.broadcasted_iota(jnp.int32, (RWKV_HEAD, HEAD_GROUP), 0)
    j_idx = lax.broadcasted_iota(jnp.int32, (RWKV_HEAD, HEAD_GROUP), 1)
    r2 = lax.broadcasted_iota(jnp.int32, (heads * 2 * c, HEAD_GROUP), 0) // (2 * c)
    c2 = lax.broadcasted_iota(jnp.int32, (heads * 2 * c, HEAD_GROUP), 1) // RWKV_HEAD
    return dict(
        bd=_head_mask(HEAD_GROUP, HEAD_GROUP),
        bd2=r2 == c2,
        before={False: s_idx < t_idx, True: s_idx > t_idx},
        upto={False: s_idx <= t_idx, True: s_idx >= t_idx},
        eye=jnp.where(s_idx == t_idx, 1.0, 0.0),
        diag=n_idx == j_idx % RWKV_HEAD)


def _scan_chunks(chains, mk):
    c = CHUNK
    bd_mask = mk["bd"]
    n = len(chains)
    rev = [ch["reverse"] for ch in chains]
    strict = [mk["before"][x] for x in rev]
    incl = [mk["upto"][x] for x in rev]
    bd = lambda t: _block_diag(t, bd_mask)
    mm = lambda a, b: jnp.dot(a, b, preferred_element_type=F32)
    each = lambda f, *cols: [f(*xs) for xs in zip(*cols)]

    tri = {x: jnp.where(mk["upto"][x][:, :c], 1.0, 0.0).astype(BF16) for x in set(rev)}
    lw = [ch["lw"] for ch in chains]
    cum = [mm(tri[x], t.astype(BF16)) for x, t in zip(rev, lw)]
    cum_edge = [t[0:1, :] if x else t[c - 1:c, :] for x, t in zip(rev, cum)]
    e_pos = each(jnp.exp, cum)
    e_neg = [jnp.exp(-t) for t in cum]
    g_edge = each(jnp.exp, cum_edge)
    e_edge = each(lambda g, e: g * e, g_edge, e_neg)
    a_t16 = [(-ch["kk"] * jnp.exp(t - l)).astype(BF16) for ch, t, l in zip(chains, cum, lw)]
    r_t = [ch["r"] * e for ch, e in zip(chains, e_pos)]
    b_t16 = [(ch["beta"] * e).astype(BF16) for ch, e in zip(chains, e_neg)]
    k_t16 = [(ch["kd"] * e).astype(BF16) for ch, e in zip(chains, e_neg)]
    b_p = [ch["beta"] * e for ch, e in zip(chains, e_edge)]
    k_p = [ch["kd"] * e for ch, e in zip(chains, e_edge)]
    v16 = [ch["v"].astype(BF16) for ch in chains]
    bd_v = each(bd, v16)

    ar = [jnp.concatenate([a, r.astype(BF16)], axis=0) for a, r in zip(a_t16, r_t)]
    pb = [_dot_nt(x, bd(b)) for x, b in zip(ar, b_t16)]
    pk = [_dot_nt(x, bd(k)) for x, k in zip(ar, k_t16)]
    p = [jnp.where(m, t[:c], 0.0) for m, t in zip(strict, pb)]
    q16 = [jnp.where(m, t[:c], 0.0).astype(BF16) for m, t in zip(strict, pk)]
    mrb16 = [jnp.where(m, t[c:], 0.0).astype(BF16) for m, t in zip(incl, pb)]
    mrk16 = [jnp.where(m, t[c:], 0.0).astype(BF16) for m, t in zip(incl, pk)]

    qv16 = [mm(a, b).astype(BF16) for a, b in zip(q16, bd_v)]

    tmat = [mk["eye"] + t for t in p]
    p16 = [t.astype(BF16) for t in p]
    pw = [mm(t, bd(t)) for t in p16]
    levels = int(math.log2(c))
    for lev in range(1, levels):
        bd_pw = [bd(t.astype(BF16)) for t in pw]
        if lev < levels - 1:
            both = [mm(jnp.concatenate([t, w], axis=0).astype(BF16), b) for t, w, b in zip(tmat, pw, bd_pw)]
            tmat = [t + x[:c] for t, x in zip(tmat, both)]
            pw = [x[c:] for x in both]
        else:
            tmat = [t + mm(t.astype(BF16), b) for t, b in zip(tmat, bd_pw)]

    t16 = [t.astype(BF16) for t in tmat]
    a16 = [mm(t, bd(a)).astype(BF16) for t, a in zip(t16, a_t16)]
    w16 = [mm(t, bd(x)).astype(BF16) for t, x in zip(t16, qv16)]

    def packed_t(b, k):
        zt = jnp.concatenate([b, k], axis=0).T
        hd = RWKV_HEAD
        return jnp.concatenate([zt[h * hd:(h + 1) * hd] for h in range(HEAD_GROUP // hd)], axis=1).astype(BF16)

    def bd2(y):
        return jnp.where(mk["bd2"], jnp.concatenate([y, y, y, y], axis=0), jnp.zeros((), y.dtype))

    zt16 = each(packed_t, b_p, k_p)
    rhs_g = [bd2(jnp.concatenate([a, jnp.zeros_like(a)], axis=0)) for a in a16]
    rhs_h = [bd2(jnp.concatenate([w, v], axis=0)) for w, v in zip(w16, v16)]
    gh = [mm(z, jnp.concatenate([g, h], axis=1)) for z, g, h in zip(zt16, rhs_g, rhs_h)]
    g_mat = [t[:, :HEAD_GROUP] + jnp.where(mk["diag"], g, 0.0) for t, g in zip(gh, g_edge)]
    h_mat = [t[:, HEAD_GROUP:] for t in gh]

    r_hat = [r + mm(m, bd(a)) for r, m, a in zip(r_t, mrb16, a16)]
    o_intra = [mm(m, bd(w)) + mm(mk_, bv) for m, w, mk_, bv in zip(mrb16, w16, mrk16, bd_v)]

    both = [mm(jnp.concatenate([r, g], axis=0).astype(BF16), bd(ch["state"].astype(BF16)))
            for r, g, ch in zip(r_hat, g_mat, chains)]
    outs = [x[:c] + o for x, o in zip(both, o_intra)]
    states = [x[c:] + h for x, h in zip(both, h_mat)]
    return outs, states


def _scan_body(rf_ref, vf_ref, kkf_ref, kdf_ref, bf_ref, lwf_ref,
               rb_ref, vb_ref, kkb_ref, kdb_ref, bb_ref, lwb_ref,
               of_ref, ob_ref, sf_ref, sb_ref, *, groups):
    @pl.when(pl.program_id(2) == 0)
    def _():
        sf_ref[...] = jnp.zeros_like(sf_ref)
        sb_ref[...] = jnp.zeros_like(sb_ref)

    mk = _scan_masks()
    n_sub = rf_ref.shape[1] // CHUNK

    def advance(i, carry):
        row_f = pl.multiple_of(i * CHUNK, CHUNK)
        row_b = pl.multiple_of((n_sub - 1 - i) * CHUNK, CHUNK)
        chains, sinks = [], []
        for gi in range(groups):
            cols = slice(gi * HEAD_GROUP, (gi + 1) * HEAD_GROUP)
            ld = lambda ref, row: ref[0, pl.ds(row, CHUNK), cols].astype(F32)
            chains.append(dict(r=ld(rf_ref, row_f), v=ld(vf_ref, row_f), kk=ld(kkf_ref, row_f),
                               kd=ld(kdf_ref, row_f), beta=ld(bf_ref, row_f), lw=ld(lwf_ref, row_f),
                               state=sf_ref[:, cols], reverse=False))
            sinks.append((of_ref, sf_ref, row_f, cols))
            chains.append(dict(r=ld(rb_ref, row_b), v=ld(vb_ref, row_b), kk=ld(kkb_ref, row_b),
                               kd=ld(kdb_ref, row_b), beta=ld(bb_ref, row_b), lw=ld(lwb_ref, row_b),
                               state=sb_ref[:, cols], reverse=True))
            sinks.append((ob_ref, sb_ref, row_b, cols))
        outs, states = _scan_chunks(chains, mk)
        for (o_ref, s_ref, row, cols), o, s in zip(sinks, outs, states):
            o_ref[0, pl.ds(row, CHUNK), cols] = o.astype(o_ref.dtype)
            s_ref[:, cols] = s
        return carry

    lax.fori_loop(0, n_sub, advance, 0)


def _rwkv_scan(r, v, kk, kd_f, kd_b, beta_f, beta_b, lw_f, lw_b, groups=SCAN_GROUPS, n_sub=SCAN_CHUNKS_PER_STEP):
    bsz, seq, _ = r.shape
    nc = seq // (CHUNK * n_sub)
    width = groups * HEAD_GROUP
    fwd = pl.BlockSpec((1, CHUNK * n_sub, width), lambda b, h, c: (b, c, h))
    bwd = pl.BlockSpec((1, CHUNK * n_sub, width), lambda b, h, c: (b, nc - 1 - c, h))
    out = jax.ShapeDtypeStruct((bsz, seq, D_RWKV), BF16)
    return pl.pallas_call(
        functools.partial(_scan_body, groups=groups),
        grid=(bsz, D_RWKV // width, nc),
        in_specs=[fwd] * 6 + [bwd] * 6,
        out_specs=[fwd, bwd],
        out_shape=[out, out],
        scratch_shapes=[pltpu.VMEM((RWKV_HEAD, width), F32), pltpu.VMEM((RWKV_HEAD, width), F32)],
        compiler_params=_cparams(("parallel", "parallel", "arbitrary")),
        name="rwkv_scan",
    )(r, v, kk, kd_f, beta_f, lw_f, r, v, kk, kd_b, beta_b, lw_b)


def _post_body(of_ref, ob_ref, bonus_ref, g_ref, lng_ref, lnb_ref, y_ref):
    ones_bd = jnp.where(_head_mask(HEAD_GROUP, HEAD_GROUP), 1.0, 0.0).astype(BF16)
    o = of_ref[0].astype(F32) + ob_ref[0].astype(F32)
    inv_n = 1.0 / RWKV_HEAD
    mu = _head_sum(o, ones_bd) * inv_n
    d = o - mu
    var = _head_sum(d * d, ones_bd) * inv_n
    on = d * lax.rsqrt(var + EPS_GN) * lng_ref[...] + lnb_ref[...]
    y_ref[0] = ((on + bonus_ref[0].astype(F32)) * g_ref[0].astype(F32)).astype(y_ref.dtype)


def _rwkv_post(o_f, o_b, bonus, g, ln_g, ln_b, tb_rows=4096):
    bsz, seq, _ = o_f.shape
    tb_rows = min(tb_rows, seq)
    blk = pl.BlockSpec((1, tb_rows, HEAD_GROUP), lambda b, t, h: (b, t, h))
    vec = pl.BlockSpec((1, HEAD_GROUP), lambda b, t, h: (0, h))
    return pl.pallas_call(
        _post_body,
        grid=(bsz, seq // tb_rows, N_HEAD_GROUPS),
        in_specs=[blk, blk, blk, blk, vec, vec],
        out_specs=blk,
        out_shape=jax.ShapeDtypeStruct((bsz, seq, D_RWKV), BF16),
        compiler_params=_cparams(("parallel", "parallel", "parallel")),
        name="rwkv_post",
    )(o_f, o_b, bonus, g, ln_g, ln_b)


def _attn_body(qc_ref, qn_ref, k_ref, v_ref, lq1_ref, lk1_ref, lq2_ref, lk2_ref, sg_ref, o_ref,
               tbl_ref, vt_ref, t_ref, e_ref, *, tq):
    head = pl.program_id(1)
    qb = pl.program_id(2)
    nq = pl.num_programs(2)
    seq = k_ref.shape[1]
    d = DIFF_HEAD
    n_chunk = seq // ATTN_KEY_CHUNK
    n_quarter = 4
    rows_q = seq // n_quarter

    def stacked(q):
        lane = lax.broadcasted_iota(jnp.int32, (tq, 2 * d), 1)
        zero = jnp.zeros((), BF16)
        return jnp.concatenate([jnp.where(lane < d, q, zero), jnp.where(lane >= d, q, zero)], axis=0)

    def score_chunk(slot, qq, blk, c):
        rows = slice(c * ATTN_KEY_CHUNK, (c + 1) * ATTN_KEY_CHUNK)
        start = seq - tq - blk * tq + c * ATTN_KEY_CHUNK
        bias = tbl_ref[pl.ds(pl.multiple_of(start, 8), ATTN_KEY_CHUNK), :]
        t_ref[slot, rows, :] = _dot_nt(k_ref[0, rows, :], qq) + jnp.concatenate([bias, bias], axis=1)

    @pl.when(qb == 0)
    def _():
        slope = jnp.exp2(jnp.full((1, 1), -8.0 / N_DIFF_HEADS, F32) * (head + 1).astype(F32)) * LOG2E
        row = lax.broadcasted_iota(jnp.int32, tbl_ref.shape, 0)
        col = lax.broadcasted_iota(jnp.int32, tbl_ref.shape, 1)
        tbl_ref[...] = -slope * jnp.abs(col - row + (seq - tq)).astype(F32)
        vt_ref[0:2 * d, :] = v_ref[0].astype(F32).T.astype(BF16)
        ones_row = lax.broadcasted_iota(jnp.int32, (ATTN_SUM_ROWS, seq), 0) == 0
        vt_ref[2 * d:, :] = jnp.where(ones_row, 1.0, 0.0).astype(BF16)
        qq0 = stacked(qc_ref[0])
        for c in range(n_chunk):
            score_chunk(0, qq0, 0, c)

    lam = (jnp.exp(jnp.sum(lq1_ref[...] * lk1_ref[...], axis=-1, keepdims=True))
           - jnp.exp(jnp.sum(lq2_ref[...] * lk2_ref[...], axis=-1, keepdims=True)) + LAMBDA_INIT)

    blk_next = jnp.minimum(qb + 1, nq - 1)

    def step(cur, nxt):
        qq_next = stacked(qn_ref[0])
        next_chunks = iter(range(n_chunk))

        def issue_scores(count):
            for _ in range(count):
                c = next(next_chunks, None)
                if c is not None:
                    score_chunk(nxt, qq_next, blk_next, c)

        m_part = None
        for qtr in range(n_quarter):
            issue_scores(n_chunk // (2 * n_quarter))
            for g in range(rows_q // ATTN_ROW_GROUP):
                r0 = qtr * rows_q + g * ATTN_ROW_GROUP
                x = t_ref[cur, r0:r0 + ATTN_ROW_GROUP, :]
                for r in range(ATTN_ROW_GROUP // 8):
                    tile = x[r * 8:(r + 1) * 8, :]
                    m_part = tile if m_part is None else jnp.maximum(m_part, tile)
        m = jnp.max(m_part, axis=0, keepdims=True)

        aug = None
        for qtr in range(n_quarter):
            issue_scores(n_chunk // (2 * n_quarter))
            for g in range(rows_q // ATTN_ROW_GROUP):
                r0 = qtr * rows_q + g * ATTN_ROW_GROUP
                e_ref[r0:r0 + ATTN_ROW_GROUP, :] = jnp.exp2(
                    t_ref[cur, r0:r0 + ATTN_ROW_GROUP, :] - m).astype(BF16)
            rows = slice(qtr * rows_q, (qtr + 1) * rows_q)
            part = jnp.dot(vt_ref[:, rows], e_ref[rows, :], preferred_element_type=F32)
            aug = part if aug is None else aug + part
        issue_scores(n_chunk)

        acc = aug[:2 * d]
        l = aug[2 * d:2 * d + 1]
        out_t = acc[:, :tq] * (1.0 / l[:, :tq]) - acc[:, tq:] * (lam / l[:, tq:])
        out = out_t.T
        out = out * lax.rsqrt(jnp.mean(out * out, axis=-1, keepdims=True) + EPS_SUBLN) * sg_ref[...]
        o_ref[0] = (out * (1.0 - LAMBDA_INIT)).astype(o_ref.dtype)

    @pl.when(qb % 2 == 0)
    def _():
        step(0, 1)

    @pl.when(qb % 2 == 1)
    def _():
        step(1, 0)


def _diff_attn(qg3, lq1, lk1, lq2, lk2, subln_g, tq=256):
    bsz, seq, _ = qg3.shape
    w = DIFF_VDIM
    nq = seq // tq
    q0 = 0
    k0 = D_DIFF // w
    v0 = 2 * D_DIFF // w
    vec = lambda n: pl.BlockSpec((1, n), lambda b, h, i: (0, 0))
    return pl.pallas_call(
        functools.partial(_attn_body, tq=tq),
        grid=(bsz, N_DIFF_HEADS, nq),
        in_specs=[pl.BlockSpec((1, tq, w), lambda b, h, i: (b, i, q0 + h)),
                  pl.BlockSpec((1, tq, w), lambda b, h, i: (b, jnp.minimum(i + 1, nq - 1), q0 + h)),
                  pl.BlockSpec((1, seq, w), lambda b, h, i: (b, 0, k0 + h)),
                  pl.BlockSpec((1, seq, w), lambda b, h, i: (b, 0, v0 + h)),
                  vec(DIFF_HEAD), vec(DIFF_HEAD), vec(DIFF_HEAD), vec(DIFF_HEAD), vec(w)],
        out_specs=pl.BlockSpec((1, tq, w), lambda b, h, i: (b, i, h)),
        out_shape=jax.ShapeDtypeStruct((bsz, seq, D_DIFF), BF16),
        scratch_shapes=[pltpu.VMEM((2 * seq - tq, tq), F32),
                        pltpu.VMEM((w + ATTN_SUM_ROWS, seq), BF16),
                        pltpu.VMEM((2, seq, 2 * tq), F32),
                        pltpu.VMEM((seq, 2 * tq), BF16)],
        compiler_params=_cparams(("parallel", "parallel", "arbitrary")),
        name="diff_attn",
    )(qg3, qg3, qg3, qg3, lq1.reshape(1, -1), lk1.reshape(1, -1), lq2.reshape(1, -1), lk2.reshape(1, -1),
      subln_g.reshape(1, -1))


def _pad_cols(t, width):
    return jnp.pad(t, ((0, 0), (0, width - t.shape[1])))


def _pad_rows(t, rows):
    return jnp.pad(t, ((0, rows - t.shape[0]), (0, 0)))


def kernel(x, attn_pre_norm, attn_post_norm, w_in, shift_prev, shift_next, decay_bias_fwd, decay_up_fwd, decay_bias_bwd, decay_up_bwd, iclr_bias_fwd, iclr_up_fwd, iclr_bias_bwd, iclr_up_bwd, gate_up, k_k, k_a, r_k, ln_x_gain, ln_x_bias, lambda_q1, lambda_k1, lambda_q2, lambda_k2, subln_gain, w_up_rwkv, w_up_diff, w_out, mlp_pre_norm, mlp_post_norm, w_mlp_in, w_mlp_out):
    bsz, seq, d = x.shape
    m = bsz * seq
    l = 0
    x2 = x.reshape(m, d)

    wt = jnp.transpose(w_in[l])
    c_dw = 3 * D_RWKV
    c_da = c_dw + RANK_LORA
    c_dg = c_da + RANK_LORA
    c_q = c_dg + RANK_GATE
    wt_lora = jnp.concatenate([_pad_rows(wt[c_dw:c_da], RANK_LORA_PAD), _pad_rows(wt[c_da:c_dg], RANK_LORA_PAD),
                               wt[c_dg:c_q]], axis=0)

    def regroup(vec):
        t = vec.reshape(3, N_HEAD_GROUPS, HEAD_GROUP)
        return jnp.transpose(t, (1, 0, 2)).reshape(N_HEAD_GROUPS, 1, 3 * HEAD_GROUP)

    def lora_vec(vec):
        t = vec.reshape(1, -1)
        return jnp.concatenate([_pad_cols(t[:, c_dw:c_da], RANK_LORA_PAD), _pad_cols(t[:, c_da:c_dg], RANK_LORA_PAD),
                                t[:, c_dg:c_q]], axis=1)

    row = lambda t: t.reshape(1, -1)
    vecs = dict(
        mu_prev3=regroup(shift_prev[l][:c_dw]), mu_next3=regroup(shift_next[l][:c_dw]),
        lmu_prev=lora_vec(shift_prev[l]), lmu_next=lora_vec(shift_next[l]),
        w0f=row(decay_bias_fwd[l]), w0b=row(decay_bias_bwd[l]), a0f=row(iclr_bias_fwd[l]), a0b=row(iclr_bias_bwd[l]),
        k_k=row(k_k[l]), k_a=row(k_a[l]), r_k=row(r_k[l]))
    mats = dict(
        w2f=_pad_rows(decay_up_fwd[l], RANK_LORA_PAD).astype(BF16), w2b=_pad_rows(decay_up_bwd[l], RANK_LORA_PAD).astype(BF16),
        a2f=_pad_rows(iclr_up_fwd[l], RANK_LORA_PAD).astype(BF16), a2b=_pad_rows(iclr_up_bwd[l], RANK_LORA_PAD).astype(BF16),
        g2=gate_up[l].astype(BF16))

    h = _prenorm(x2, attn_pre_norm[l])
    col_scale = jnp.ones((1, N_QG), F32).at[:, :D_DIFF].set(DIFF_HEAD ** -0.5 * LOG2E)
    rkv3 = _matmul(h, wt, BF16, tm=2048, tn=512, bt_rows=(0, c_dw),
                   name="in_proj_rkv").reshape(bsz, seq, 3 * D_RWKV)
    lora3 = _matmul(h, wt_lora, BF16, tm=1024, tn=LORA_W, bt_rows=(0, LORA_W),
                    name="in_proj_lora").reshape(bsz, seq, LORA_W)
    qg = _matmul(h, wt, BF16, tm=2048, tn=512, col_scale=col_scale, bt_rows=(c_q, N_QG), name="in_proj_qg")

    r, v, kk, kd_f, kd_b, beta_f, beta_b, lw_f, lw_b, g, bonus = _rwkv_prep(rkv3, lora3, vecs, mats)
    o_f, o_b = _rwkv_scan(r, v, kk, kd_f, kd_b, beta_f, beta_b, lw_f, lw_b)
    y_a = _rwkv_post(o_f, o_b, bonus, g, row(ln_x_gain[l]), row(ln_x_bias[l]))

    y_b = _diff_attn(qg.reshape(bsz, seq, N_QG), lambda_q1[l], lambda_k1[l], lambda_q2[l], lambda_k2[l],
                     subln_gain[l])

    mixed = _upgate(y_a.reshape(m, D_RWKV), y_b.reshape(m, D_DIFF),
                    w_up_rwkv[l], w_up_diff[l], qg)
    z = _matmul(mixed, w_out[l], BF16, tm=2048, tn=512, name="out_proj")
    x1, h2 = _resnorm(x2, z, attn_post_norm[l], mlp_pre_norm[l])

    u = _matmul(h2, w_mlp_in[l], BF16, tm=2048, tn=512, act="relu2", name="mlp_in")
    z2 = _matmul_ktiled(u, w_mlp_out[l], BF16, tm=1024, tn=1024, tk=2048, name="mlp_out")
    out = _resnorm(x1, z2, mlp_post_norm[l])
    return out.reshape(bsz, seq, d)
```

```python
import functools
import math

import jax
import jax.numpy as jnp
from jax import lax
from jax.experimental import pallas as pl
from jax.experimental.pallas import tpu as pltpu

F32 = jnp.float32
BF16 = jnp.bfloat16

D_MODEL = 4096
D_RWKV = D_MODEL // 2
RWKV_HEAD = 64
RANK_LORA = 96
RANK_LORA_PAD = 128
RANK_GATE = 256
D_DIFF = D_MODEL // 2
DIFF_HEAD = 64
N_DIFF_HEADS = D_DIFF // (2 * DIFF_HEAD)
DIFF_VDIM = 2 * DIFF_HEAD
D_FF = 4 * D_MODEL
EPS_RMS = 1e-6
EPS_GN = 64e-5
EPS_SUBLN = 1e-5
LAMBDA_INIT = 0.8 - 0.6 * math.exp(-0.3 * 0)
LOG2E = math.log2(math.e)

HEAD_GROUP = 256
N_HEAD_GROUPS = D_RWKV // HEAD_GROUP
LORA_W = 2 * RANK_LORA_PAD + RANK_GATE
COL_QKV = 3 * D_RWKV
COL_GATE = COL_QKV + 3 * D_DIFF
N_PROJ = COL_GATE + 2 * D_MODEL

CHUNK = 64
SCAN_GROUPS = 8
SCAN_CHUNKS_PER_STEP = 4
ATTN_SUM_ROWS = 16
ATTN_KEY_CHUNK = 512
ATTN_ROW_GROUP = 32
VMEM_LIMIT = 48 * 1024 * 1024


def _cparams(sem):
    return pltpu.CompilerParams(dimension_semantics=sem, vmem_limit_bytes=VMEM_LIMIT)


def _rms(t, g):
    return t * lax.rsqrt(jnp.mean(t * t, axis=-1, keepdims=True) + EPS_RMS) * g


def _prenorm_body(x_ref, g_ref, o_ref):
    o_ref[...] = _rms(x_ref[...], g_ref[...]).astype(o_ref.dtype)


def _prenorm(x, g, tr=256):
    m, d = x.shape
    return pl.pallas_call(
        _prenorm_body,
        grid=(m // tr,),
        in_specs=[pl.BlockSpec((tr, d), lambda i: (i, 0)), pl.BlockSpec((1, d), lambda i: (0, 0))],
        out_specs=pl.BlockSpec((tr, d), lambda i: (i, 0)),
        out_shape=jax.ShapeDtypeStruct((m, d), BF16),
        compiler_params=_cparams(("parallel",)),
        name="prenorm",
    )(x, g.reshape(1, d))


def _resnorm_body(x_ref, z_ref, g_ref, g2_ref, o_ref, h_ref):
    y = x_ref[...] + _rms(z_ref[...].astype(F32), g_ref[...])
    o_ref[...] = y
    if h_ref is not None:
        h_ref[...] = _rms(y, g2_ref[...]).astype(h_ref.dtype)


def _resnorm(x, z, g, g_next=None, tr=256):
    m, d = x.shape
    row = pl.BlockSpec((tr, d), lambda i: (i, 0))
    vec = pl.BlockSpec((1, d), lambda i: (0, 0))
    if g_next is None:
        body = lambda x_ref, z_ref, g_ref, o_ref: _resnorm_body(x_ref, z_ref, g_ref, None, o_ref, None)
        return pl.pallas_call(
            body, grid=(m // tr,), in_specs=[row, row, vec], out_specs=row,
            out_shape=jax.ShapeDtypeStruct((m, d), F32),
            compiler_params=_cparams(("parallel",)), name="resnorm_out",
        )(x, z, g.reshape(1, d))
    return pl.pallas_call(
        _resnorm_body, grid=(m // tr,), in_specs=[row, row, vec, vec], out_specs=[row, row],
        out_shape=[jax.ShapeDtypeStruct((m, d), F32), jax.ShapeDtypeStruct((m, d), BF16)],
        compiler_params=_cparams(("parallel",)), name="resnorm_mid",
    )(x, z, g.reshape(1, d), g_next.reshape(1, d))


def _mm_body(a_ref, b_ref, *rest, act, b_transposed, scaled):
    o_ref = rest[-1]
    dims = (((1,), (1 if b_transposed else 0,)), ((), ()))
    acc = lax.dot_general(a_ref[...], b_ref[...].astype(BF16), dims, preferred_element_type=F32)
    if act == "relu2":
        acc = jnp.square(jnp.maximum(acc, 0.0))
    if scaled:
        acc = acc * rest[0][...]
    o_ref[...] = acc.astype(o_ref.dtype)


def _matmul(a, b, out_dtype, tm, tn, act=None, col_scale=None, bt_rows=None, name="matmul"):
    m, k = a.shape
    b_transposed = bt_rows is not None
    ij = lambda g0, g1: (g0, g1)
    if b_transposed:
        n = sum(cnt for _, cnt in bt_rows)
        (first0, cnt0), (first1, _) = bt_rows[0], bt_rows[-1]
        tiles0 = cnt0 // tn

        def row_start(j):
            return j * tn + jnp.where(j < tiles0, first0, first1 - cnt0) if len(bt_rows) > 1 else first0 + j * tn
        b_block = (tn, k)
        b_spec = pl.BlockSpec((pl.Element(tn), pl.Element(k)),
                              lambda g0, g1: (pl.multiple_of(row_start(ij(g0, g1)[1]), 16), 0))
    else:
        n = b.shape[1]
        b_block = (k, tn)
        b_spec = pl.BlockSpec(b_block, lambda g0, g1: (0, ij(g0, g1)[1]))
    a_mode = pl.Buffered(1) if n // tn >= 4 else None
    in_specs = [pl.BlockSpec((tm, k), lambda g0, g1: (ij(g0, g1)[0], 0), pipeline_mode=a_mode), b_spec]
    args = (a, b)
    if col_scale is not None:
        in_specs.append(pl.BlockSpec((1, tn), lambda g0, g1: (0, ij(g0, g1)[1])))
        args = (a, b, col_scale)
    body = functools.partial(_mm_body, act=act, b_transposed=b_transposed, scaled=col_scale is not None)
    return pl.pallas_call(
        body,
        grid=(m // tm, n // tn),
        in_specs=in_specs,
        out_specs=pl.BlockSpec((tm, tn), lambda g0, g1: ij(g0, g1)),
        out_shape=jax.ShapeDtypeStruct((m, n), out_dtype),
        compiler_params=_cparams(("parallel", "arbitrary")),
        name=name,
    )(*args)


def _mmk_body(a_ref, b_ref, o_ref, acc_ref):
    kk = pl.program_id(2)

    @pl.when(kk == 0)
    def _():
        acc_ref[...] = jnp.zeros_like(acc_ref)

    acc_ref[...] += jnp.dot(a_ref[...], b_ref[...].astype(BF16), preferred_element_type=F32)

    @pl.when(kk == pl.num_programs(2) - 1)
    def _():
        o_ref[...] = acc_ref[...].astype(o_ref.dtype)


def _matmul_ktiled(a, b, out_dtype, tm, tn, tk, name="matmul_k"):
    m, k = a.shape
    _, n = b.shape
    return pl.pallas_call(
        _mmk_body,
        grid=(m // tm, n // tn, k // tk),
        in_specs=[pl.BlockSpec((tm, tk), lambda i, j, q: (i, q)), pl.BlockSpec((tk, tn), lambda i, j, q: (q, j))],
        out_specs=pl.BlockSpec((tm, tn), lambda i, j, q: (i, j)),
        out_shape=jax.ShapeDtypeStruct((m, n), out_dtype),
        scratch_shapes=[pltpu.VMEM((tm, tn), F32)],
        compiler_params=_cparams(("parallel", "parallel", "arbitrary")),
        name=name,
    )(a, b)


def _sigmoid(t):
    return 0.5 * jnp.tanh(0.5 * t) + 0.5


def _upgate_body(ya_ref, yb_ref, wa_ref, wb_ref, ga_ref, gb_ref, o_ref):
    acc_a = jnp.dot(ya_ref[...], wa_ref[...].astype(BF16), preferred_element_type=F32)
    acc_b = jnp.dot(yb_ref[...], wb_ref[...].astype(BF16), preferred_element_type=F32)
    ga = _sigmoid(ga_ref[...].astype(F32))
    gb = _sigmoid(gb_ref[...].astype(F32))
    o_ref[...] = (ga * acc_a + gb * acc_b).astype(o_ref.dtype)


def _upgate(ya, yb, wa, wb, qg, tm=1024, tn=512):
    m, k = ya.shape
    n = wa.shape[1]
    ga0 = COL_GATE // tn
    gb0 = (COL_GATE + D_MODEL) // tn
    return pl.pallas_call(
        _upgate_body,
        grid=(m // tm, n // tn),
        in_specs=[
            pl.BlockSpec((tm, k), lambda i, j: (i, 0)),
            pl.BlockSpec((tm, k), lambda i, j: (i, 0)),
            pl.BlockSpec((k, tn), lambda i, j: (0, j)),
            pl.BlockSpec((k, tn), lambda i, j: (0, j)),
            pl.BlockSpec((tm, tn), lambda i, j: (i, ga0 + j)),
            pl.BlockSpec((tm, tn), lambda i, j: (i, gb0 + j)),
        ],
        out_specs=pl.BlockSpec((tm, tn), lambda i, j: (i, j)),
        out_shape=jax.ShapeDtypeStruct((m, n), BF16),
        compiler_params=_cparams(("parallel", "arbitrary")),
        name="upgate",
    )(ya, yb, wa, wb, qg, qg)


def _head_mask(rows, cols):
    r = lax.broadcasted_iota(jnp.int32, (rows, cols), 0) // RWKV_HEAD
    c = lax.broadcasted_iota(jnp.int32, (rows, cols), 1) // RWKV_HEAD
    return r == c


def _split_bf16(t):
    hi = t.astype(BF16)
    lo = (t - hi.astype(F32)).astype(BF16)
    return hi, lo


def _head_sum(t, ones_bd):
    hi, lo = _split_bf16(t)
    return (jnp.dot(hi, ones_bd, preferred_element_type=F32)
            + jnp.dot(lo, ones_bd, preferred_element_type=F32))


def _prep_body(r_ref, k_ref, v_ref, lo_ref,
               rp_ref, kp_ref, vp_ref, lop_ref, rn_ref, kn_ref, vn_ref, lon_ref,
               mup_ref, mun_ref, lmup_ref, lmun_ref,
               w0f_ref, w0b_ref, a0f_ref, a0b_ref, kk_ref, ka_ref, rk_ref,
               w2f_ref, w2b_ref, a2f_ref, a2b_ref, g2_ref,
               ro_ref, vo_ref, kko_ref, kdf_ref, kdb_ref, bf_ref, bb_ref, lwf_ref, lwb_ref,
               g_ref, bonus_ref,
               tw_s, da_s, sg_s, *, tb_rows):
    tb = pl.program_id(1)
    hg = pl.program_id(2)
    first = tb == 0
    last = tb == pl.num_programs(1) - 1

    def shift_mix(x_ref, p_ref, n_ref, mup, mun):
        x = x_ref[0].astype(F32)
        width = x.shape[1]
        prev_row = jnp.where(first, 0.0, p_ref[0][7:8, :].astype(F32))
        next_row = jnp.where(last, 0.0, n_ref[0][0:1, :].astype(F32))
        row = lax.broadcasted_iota(jnp.int32, (tb_rows, width), 0)
        xp = jnp.where(row == 0, prev_row, pltpu.roll(x, 1, axis=0))
        xn = jnp.where(row == tb_rows - 1, next_row, pltpu.roll(x, tb_rows - 1, axis=0))
        return (1.0 - mup - mun) * x + mup * xp + mun * xn

    @pl.when(hg == 0)
    def _():
        lo = shift_mix(lo_ref, lop_ref, lon_ref, lmup_ref[...], lmun_ref[...])
        tw_s[...] = jnp.tanh(lo[:, :RANK_LORA_PAD]).astype(BF16)
        da_s[...] = lo[:, RANK_LORA_PAD:2 * RANK_LORA_PAD].astype(BF16)
        sg_s[...] = _sigmoid(lo[:, 2 * RANK_LORA_PAD:]).astype(BF16)

    mu_p = mup_ref[...]
    mu_n = mun_ref[...]
    r = shift_mix(r_ref, rp_ref, rn_ref, mu_p[:, 0:HEAD_GROUP], mu_n[:, 0:HEAD_GROUP])
    k = shift_mix(k_ref, kp_ref, kn_ref, mu_p[:, HEAD_GROUP:2 * HEAD_GROUP], mu_n[:, HEAD_GROUP:2 * HEAD_GROUP])
    v = shift_mix(v_ref, vp_ref, vn_ref, mu_p[:, 2 * HEAD_GROUP:], mu_n[:, 2 * HEAD_GROUP:])

    ones_bd = jnp.where(_head_mask(HEAD_GROUP, HEAD_GROUP), 1.0, 0.0).astype(BF16)
    kk = k * kk_ref[...]
    kk = kk * lax.rsqrt(jnp.maximum(_head_sum(kk * kk, ones_bd), 1e-24))

    tw = tw_s[...]
    da = da_s[...]

    def direction(w0_ref, w2_ref, a0_ref, a2_ref):
        dec = w0_ref[...] + jnp.dot(tw, w2_ref[...], preferred_element_type=F32)
        z = -dec
        softplus = jnp.maximum(z, 0.0) + jnp.log(1.0 + jnp.exp(-jnp.abs(z)))
        lw = -jnp.exp(-softplus - 0.5)
        a = _sigmoid(a0_ref[...] + jnp.dot(da, a2_ref[...], preferred_element_type=F32))
        kd = k * (1.0 + (a - 1.0) * ka_ref[...])
        return lw, a, kd

    lw_f, a_f, kd_f = direction(w0f_ref, w2f_ref, a0f_ref, a2f_ref)
    lw_b, a_b, kd_b = direction(w0b_ref, w2b_ref, a0b_ref, a2b_ref)

    bonus = _head_sum(r * (0.5 * (kd_f + kd_b)) * rk_ref[...], ones_bd) * v
    g = jnp.dot(sg_s[...], g2_ref[...], preferred_element_type=F32)

    ro_ref[0] = r.astype(ro_ref.dtype)
    vo_ref[0] = v.astype(vo_ref.dtype)
    kko_ref[0] = kk.astype(kko_ref.dtype)
    kdf_ref[0] = kd_f.astype(kdf_ref.dtype)
    kdb_ref[0] = kd_b.astype(kdb_ref.dtype)
    bf_ref[0] = (kk * a_f).astype(bf_ref.dtype)
    bb_ref[0] = (kk * a_b).astype(bb_ref.dtype)
    lwf_ref[0] = lw_f.astype(lwf_ref.dtype)
    lwb_ref[0] = lw_b.astype(lwb_ref.dtype)
    g_ref[0] = g.astype(g_ref.dtype)
    bonus_ref[0] = bonus.astype(bonus_ref.dtype)


def _rwkv_prep(rkv3, lora3, vecs, mats, tb_rows=2048):
    bsz, seq, _ = rkv3.shape
    tb_rows = min(tb_rows, seq)
    ntb = seq // tb_rows
    hgw = HEAD_GROUP
    n8 = seq // 8
    r0, k0, v0 = 0, D_RWKV // hgw, 2 * D_RWKV // hgw
    lo0 = 0

    def main(c0):
        return pl.BlockSpec((1, tb_rows, hgw), lambda b, t, h: (b, t, c0 + h))

    def prev(c0):
        return pl.BlockSpec((1, 8, hgw), lambda b, t, h: (b, jnp.maximum(t * (tb_rows // 8) - 1, 0), c0 + h))

    def nxt(c0):
        return pl.BlockSpec((1, 8, hgw), lambda b, t, h: (b, jnp.minimum((t + 1) * (tb_rows // 8), n8 - 1), c0 + h))

    lo_main = pl.BlockSpec((1, tb_rows, LORA_W), lambda b, t, h: (b, t, lo0))
    lo_prev = pl.BlockSpec((1, 8, LORA_W), lambda b, t, h: (b, jnp.maximum(t * (tb_rows // 8) - 1, 0), lo0))
    lo_next = pl.BlockSpec((1, 8, LORA_W), lambda b, t, h: (b, jnp.minimum((t + 1) * (tb_rows // 8), n8 - 1), lo0))

    def hvec():
        return pl.BlockSpec((1, hgw), lambda b, t, h: (0, h))

    def hmat(rows):
        return pl.BlockSpec((rows, hgw), lambda b, t, h: (0, h))

    mu3 = pl.BlockSpec((None, 1, 3 * hgw), lambda b, t, h: (h, 0, 0))
    full = lambda shape: pl.BlockSpec(shape, lambda b, t, h: (0,) * len(shape))

    in_specs = [main(r0), main(k0), main(v0), lo_main,
                prev(r0), prev(k0), prev(v0), lo_prev, nxt(r0), nxt(k0), nxt(v0), lo_next,
                mu3, mu3, full((1, LORA_W)), full((1, LORA_W)),
                hvec(), hvec(), hvec(), hvec(), hvec(), hvec(), hvec(),
                hmat(RANK_LORA_PAD), hmat(RANK_LORA_PAD), hmat(RANK_LORA_PAD), hmat(RANK_LORA_PAD),
                hmat(RANK_GATE)]
    out_block = pl.BlockSpec((1, tb_rows, hgw), lambda b, t, h: (b, t, h))
    shp = lambda dt: jax.ShapeDtypeStruct((bsz, seq, D_RWKV), dt)
    out_dtypes = [BF16] * 11
    return pl.pallas_call(
        functools.partial(_prep_body, tb_rows=tb_rows),
        grid=(bsz, ntb, N_HEAD_GROUPS),
        in_specs=in_specs,
        out_specs=[out_block] * len(out_dtypes),
        out_shape=[shp(dt) for dt in out_dtypes],
        scratch_shapes=[pltpu.VMEM((tb_rows, RANK_LORA_PAD), BF16),
                        pltpu.VMEM((tb_rows, RANK_LORA_PAD), BF16),
                        pltpu.VMEM((tb_rows, RANK_GATE), BF16)],
        compiler_params=_cparams(("parallel", "parallel", "arbitrary")),
        name="rwkv_prep",
    )(rkv3, rkv3, rkv3, lora3, rkv3, rkv3, rkv3, lora3, rkv3, rkv3, rkv3, lora3,
      vecs["mu_prev3"], vecs["mu_next3"], vecs["lmu_prev"], vecs["lmu_next"],
      vecs["w0f"], vecs["w0b"], vecs["a0f"], vecs["a0b"], vecs["k_k"], vecs["k_a"], vecs["r_k"],
      mats["w2f"], mats["w2b"], mats["a2f"], mats["a2b"], mats["g2"])


def _dot_nt(a, b):
    return lax.dot_general(a, b, (((1,), (1,)), ((), ())), preferred_element_type=F32)


def _block_diag(t, mask):
    return jnp.where(mask, jnp.concatenate([t, t, t, t], axis=0), jnp.zeros((), t.dtype))


def _scan_masks():
    c = CHUNK
    heads = HEAD_GROUP // RWKV_HEAD
    t_idx = lax.broadcasted_iota(jnp.int32, (c, HEAD_GROUP), 0)
    s_idx = lax.broadcasted_iota(jnp.int32, (c, HEAD_GROUP), 1) % c
    n_idx = lax.broadcasted_iota(jnp.int32, (RWKV_HEAD, HEAD_GROUP), 0)
    j_idx = lax.broadcasted_iota(jnp.int32, (RWKV_HEAD, HEAD_GROUP), 1)
    r2 = lax.broadcasted_iota(jnp.int32, (heads * 2 * c, HEAD_GROUP), 0) // (2 * c)
    c2 = lax.broadcasted_iota(jnp.int32, (heads * 2 * c, HEAD_GROUP), 1) // RWKV_HEAD
    return dict(
        bd=_head_mask(HEAD_GROUP, HEAD_GROUP),
        bd2=r2 == c2,
        before={False: s_idx < t_idx, True: s_idx > t_idx},
        upto={False: s_idx <= t_idx, True: s_idx >= t_idx},
        eye=jnp.where(s_idx == t_idx, 1.0, 0.0),
        diag=n_idx == j_idx % RWKV_HEAD)


def _scan_chunks(chains, mk):
    c = CHUNK
    bd_mask = mk["bd"]
    n = len(chains)
    rev = [ch["reverse"] for ch in chains]
    strict = [mk["before"][x] for x in rev]
    incl = [mk["upto"][x] for x in rev]
    bd = lambda t: _block_diag(t, bd_mask)
    mm = lambda a, b: jnp.dot(a, b, preferred_element_type=F32)
    each = lambda f, *cols: [f(*xs) for xs in zip(*cols)]

    tri = {x: jnp.where(mk["upto"][x][:, :c], 1.0, 0.0).astype(BF16) for x in set(rev)}
    lw = [ch["lw"] for ch in chains]
    cum = [mm(tri[x], t.astype(BF16)) for x, t in zip(rev, lw)]
    cum_edge = [t[0:1, :] if x else t[c - 1:c, :] for x, t in zip(rev, cum)]
    e_pos = each(jnp.exp, cum)
    e_neg = [jnp.exp(-t) for t in cum]
    g_edge = each(jnp.exp, cum_edge)
    e_edge = each(lambda g, e: g * e, g_edge, e_neg)
    a_t16 = [(-ch["kk"] * jnp.exp(t - l)).astype(BF16) for ch, t, l in zip(chains, cum, lw)]
    r_t = [ch["r"] * e for ch, e in zip(chains, e_pos)]
    b_t16 = [(ch["beta"] * e).astype(BF16) for ch, e in zip(chains, e_neg)]
    k_t16 = [(ch["kd"] * e).astype(BF16) for ch, e in zip(chains, e_neg)]
    b_p = [ch["beta"] * e for ch, e in zip(chains, e_edge)]
    k_p = [ch["kd"] * e for ch, e in zip(chains, e_edge)]
    v16 = [ch["v"].astype(BF16) for ch in chains]
    bd_v = each(bd, v16)

    ar = [jnp.concatenate([a, r.astype(BF16)], axis=0) for a, r in zip(a_t16, r_t)]
    pb = [_dot_nt(x, bd(b)) for x, b in zip(ar, b_t16)]
    pk = [_dot_nt(x, bd(k)) for x, k in zip(ar, k_t16)]
    p = [jnp.where(m, t[:c], 0.0) for m, t in zip(strict, pb)]
    q16 = [jnp.where(m, t[:c], 0.0).astype(BF16) for m, t in zip(strict, pk)]
    mrb16 = [jnp.where(m, t[c:], 0.0).astype(BF16) for m, t in zip(incl, pb)]
    mrk16 = [jnp.where(m, t[c:], 0.0).astype(BF16) for m, t in zip(incl, pk)]

    qv16 = [mm(a, b).astype(BF16) for a, b in zip(q16, bd_v)]

    tmat = [mk["eye"] + t for t in p]
    p16 = [t.astype(BF16) for t in p]
    pw = [mm(t, bd(t)) for t in p16]
    levels = int(math.log2(c))
    for lev in range(1, levels):
        bd_pw = [bd(t.astype(BF16)) for t in pw]
        if lev < levels - 1:
            both = [mm(jnp.concatenate([t, w], axis=0).astype(BF16), b) for t, w, b in zip(tmat, pw, bd_pw)]
            tmat = [t + x[:c] for t, x in zip(tmat, both)]
            pw = [x[c:] for x in both]
        else:
            tmat = [t + mm(t.astype(BF16), b) for t, b in zip(tmat, bd_pw)]

    t16 = [t.astype(BF16) for t in tmat]
    a16 = [mm(t, bd(a)).astype(BF16) for t, a in zip(t16, a_t16)]
    w16 = [mm(t, bd(x)).astype(BF16) for t, x in zip(t16, qv16)]

    def packed_t(b, k):
        zt = jnp.concatenate([b, k], axis=0).T
        hd = RWKV_HEAD
        return jnp.concatenate([zt[h * hd:(h + 1) * hd] for h in range(HEAD_GROUP // hd)], axis=1).astype(BF16)

    def bd2(y):
        return jnp.where(mk["bd2"], jnp.concatenate([y, y, y, y], axis=0), jnp.zeros((), y.dtype))

    zt16 = each(packed_t, b_p, k_p)
    rhs_g = [bd2(jnp.concatenate([a, jnp.zeros_like(a)], axis=0)) for a in a16]
    rhs_h = [bd2(jnp.concatenate([w, v], axis=0)) for w, v in zip(w16, v16)]
    gh = [mm(z, jnp.concatenate([g, h], axis=1)) for z, g, h in zip(zt16, rhs_g, rhs_h)]
    g_mat = [t[:, :HEAD_GROUP] + jnp.where(mk["diag"], g, 0.0) for t, g in zip(gh, g_edge)]
    h_mat = [t[:, HEAD_GROUP:] for t in gh]

    r_hat = [r + mm(m, bd(a)) for r, m, a in zip(r_t, mrb16, a16)]
    o_intra = [mm(m, bd(w)) + mm(mk_, bv) for m, w, mk_, bv in zip(mrb16, w16, mrk16, bd_v)]

    both = [mm(jnp.concatenate([r, g], axis=0).astype(BF16), bd(ch["state"].astype(BF16)))
            for r, g, ch in zip(r_hat, g_mat, chains)]
    outs = [x[:c] + o for x, o in zip(both, o_intra)]
    states = [x[c:] + h for x, h in zip(both, h_mat)]
    return outs, states


def _scan_body(rf_ref, vf_ref, kkf_ref, kdf_ref, bf_ref, lwf_ref,
               rb_ref, vb_ref, kkb_ref, kdb_ref, bb_ref, lwb_ref,
               of_ref, ob_ref, sf_ref, sb_ref, *, groups):
    @pl.when(pl.program_id(2) == 0)
    def _():
        sf_ref[...] = jnp.zeros_like(sf_ref)
        sb_ref[...] = jnp.zeros_like(sb_ref)

    mk = _scan_masks()
    n_sub = rf_ref.shape[1] // CHUNK

    def advance(i, carry):
        row_f = pl.multiple_of(i * CHUNK, CHUNK)
        row_b = pl.multiple_of((n_sub - 1 - i) * CHUNK, CHUNK)
        chains, sinks = [], []
        for gi in range(groups):
            cols = slice(gi * HEAD_GROUP, (gi + 1) * HEAD_GROUP)
            ld = lambda ref, row: ref[0, pl.ds(row, CHUNK), cols].astype(F32)
            chains.append(dict(r=ld(rf_ref, row_f), v=ld(vf_ref, row_f), kk=ld(kkf_ref, row_f),
                               kd=ld(kdf_ref, row_f), beta=ld(bf_ref, row_f), lw=ld(lwf_ref, row_f),
                               state=sf_ref[:, cols], reverse=False))
            sinks.append((of_ref, sf_ref, row_f, cols))
            chains.append(dict(r=ld(rb_ref, row_b), v=ld(vb_ref, row_b), kk=ld(kkb_ref, row_b),
                               kd=ld(kdb_ref, row_b), beta=ld(bb_ref, row_b), lw=ld(lwb_ref, row_b),
                               state=sb_ref[:, cols], reverse=True))
            sinks.append((ob_ref, sb_ref, row_b, cols))
        outs, states = _scan_chunks(chains, mk)
        for (o_ref, s_ref, row, cols), o, s in zip(sinks, outs, states):
            o_ref[0, pl.ds(row, CHUNK), cols] = o.astype(o_ref.dtype)
            s_ref[:, cols] = s
        return carry

    lax.fori_loop(0, n_sub, advance, 0)


def _rwkv_scan(r, v, kk, kd_f, kd_b, beta_f, beta_b, lw_f, lw_b, groups=SCAN_GROUPS, n_sub=SCAN_CHUNKS_PER_STEP):
    bsz, seq, _ = r.shape
    nc = seq // (CHUNK * n_sub)
    width = groups * HEAD_GROUP
    fwd = pl.BlockSpec((1, CHUNK * n_sub, width), lambda b, h, c: (b, c, h))
    bwd = pl.BlockSpec((1, CHUNK * n_sub, width), lambda b, h, c: (b, nc - 1 - c, h))
    out = jax.ShapeDtypeStruct((bsz, seq, D_RWKV), BF16)
    return pl.pallas_call(
        functools.partial(_scan_body, groups=groups),
        grid=(bsz, D_RWKV // width, nc),
        in_specs=[fwd] * 6 + [bwd] * 6,
        out_specs=[fwd, bwd],
        out_shape=[out, out],
        scratch_shapes=[pltpu.VMEM((RWKV_HEAD, width), F32), pltpu.VMEM((RWKV_HEAD, width), F32)],
        compiler_params=_cparams(("parallel", "parallel", "arbitrary")),
        name="rwkv_scan",
    )(r, v, kk, kd_f, beta_f, lw_f, r, v, kk, kd_b, beta_b, lw_b)


def _post_body(of_ref, ob_ref, bonus_ref, g_ref, lng_ref, lnb_ref, y_ref):
    ones_bd = jnp.where(_head_mask(HEAD_GROUP, HEAD_GROUP), 1.0, 0.0).astype(BF16)
    o = of_ref[0].astype(F32) + ob_ref[0].astype(F32)
    inv_n = 1.0 / RWKV_HEAD
    mu = _head_sum(o, ones_bd) * inv_n
    d = o - mu
    var = _head_sum(d * d, ones_bd) * inv_n
    on = d * lax.rsqrt(var + EPS_GN) * lng_ref[...] + lnb_ref[...]
    y_ref[0] = ((on + bonus_ref[0].astype(F32)) * g_ref[0].astype(F32)).astype(y_ref.dtype)


def _rwkv_post(o_f, o_b, bonus, g, ln_g, ln_b, tb_rows=4096):
    bsz, seq, _ = o_f.shape
    tb_rows = min(tb_rows, seq)
    blk = pl.BlockSpec((1, tb_rows, HEAD_GROUP), lambda b, t, h: (b, t, h))
    vec = pl.BlockSpec((1, HEAD_GROUP), lambda b, t, h: (0, h))
    return pl.pallas_call(
        _post_body,
        grid=(bsz, seq // tb_rows, N_HEAD_GROUPS),
        in_specs=[blk, blk, blk, blk, vec, vec],
        out_specs=blk,
        out_shape=jax.ShapeDtypeStruct((bsz, seq, D_RWKV), BF16),
        compiler_params=_cparams(("parallel", "parallel", "parallel")),
        name="rwkv_post",
    )(o_f, o_b, bonus, g, ln_g, ln_b)


def _attn_body(qc_ref, qn_ref, k_ref, v_ref, lq1_ref, lk1_ref, lq2_ref, lk2_ref, sg_ref, o_ref,
               tbl_ref, vt_ref, t_ref, e_ref, *, tq):
    head = pl.program_id(1)
    qb = pl.program_id(2)
    nq = pl.num_programs(2)
    seq = k_ref.shape[1]
    d = DIFF_HEAD
    n_chunk = seq // ATTN_KEY_CHUNK
    n_quarter = 4
    rows_q = seq // n_quarter

    def stacked(q):
        lane = lax.broadcasted_iota(jnp.int32, (tq, 2 * d), 1)
        zero = jnp.zeros((), BF16)
        return jnp.concatenate([jnp.where(lane < d, q, zero), jnp.where(lane >= d, q, zero)], axis=0)

    def score_chunk(slot, qq, blk, c):
        rows = slice(c * ATTN_KEY_CHUNK, (c + 1) * ATTN_KEY_CHUNK)
        start = seq - tq - blk * tq + c * ATTN_KEY_CHUNK
        bias = tbl_ref[pl.ds(pl.multiple_of(start, 8), ATTN_KEY_CHUNK), :]
        t_ref[slot, rows, :] = _dot_nt(k_ref[0, rows, :], qq) + jnp.concatenate([bias, bias], axis=1)

    @pl.when(qb == 0)
    def _():
        slope = jnp.exp2(jnp.full((1, 1), -8.0 / N_DIFF_HEADS, F32) * (head + 1).astype(F32)) * LOG2E
        row = lax.broadcasted_iota(jnp.int32, tbl_ref.shape, 0)
        col = lax.broadcasted_iota(jnp.int32, tbl_ref.shape, 1)
        tbl_ref[...] = -slope * jnp.abs(col - row + (seq - tq)).astype(F32)
        vt_ref[0:2 * d, :] = v_ref[0].astype(F32).T.astype(BF16)
        ones_row = lax.broadcasted_iota(jnp.int32, (ATTN_SUM_ROWS, seq), 0) == 0
        vt_ref[2 * d:, :] = jnp.where(ones_row, 1.0, 0.0).astype(BF16)
        qq0 = stacked(qc_ref[0])
        for c in range(n_chunk):
            score_chunk(0, qq0, 0, c)

    lam = (jnp.exp(jnp.sum(lq1_ref[...] * lk1_ref[...], axis=-1, keepdims=True))
           - jnp.exp(jnp.sum(lq2_ref[...] * lk2_ref[...], axis=-1, keepdims=True)) + LAMBDA_INIT)

    blk_next = jnp.minimum(qb + 1, nq - 1)

    def step(cur, nxt):
        qq_next = stacked(qn_ref[0])
        next_chunks = iter(range(n_chunk))

        def issue_scores(count):
            for _ in range(count):
                c = next(next_chunks, None)
                if c is not None:
                    score_chunk(nxt, qq_next, blk_next, c)

        m_part = None
        for qtr in range(n_quarter):
            issue_scores(n_chunk // (2 * n_quarter))
            for g in range(rows_q // ATTN_ROW_GROUP):
                r0 = qtr * rows_q + g * ATTN_ROW_GROUP
                x = t_ref[cur, r0:r0 + ATTN_ROW_GROUP, :]
                for r in range(ATTN_ROW_GROUP // 8):
                    tile = x[r * 8:(r + 1) * 8, :]
                    m_part = tile if m_part is None else jnp.maximum(m_part, tile)
        m = jnp.max(m_part, axis=0, keepdims=True)

        aug = None
        for qtr in range(n_quarter):
            issue_scores(n_chunk // (2 * n_quarter))
            for g in range(rows_q // ATTN_ROW_GROUP):
                r0 = qtr * rows_q + g * ATTN_ROW_GROUP
                e_ref[r0:r0 + ATTN_ROW_GROUP, :] = jnp.exp2(
                    t_ref[cur, r0:r0 + ATTN_ROW_GROUP, :] - m).astype(BF16)
            rows = slice(qtr * rows_q, (qtr + 1) * rows_q)
            part = jnp.dot(vt_ref[:, rows], e_ref[rows, :], preferred_element_type=F32)
            aug = part if aug is None else aug + part
        issue_scores(n_chunk)

        acc = aug[:2 * d]
        l = aug[2 * d:2 * d + 1]
        out_t = acc[:, :tq] * (1.0 / l[:, :tq]) - acc[:, tq:] * (lam / l[:, tq:])
        out = out_t.T
        out = out * lax.rsqrt(jnp.mean(out * out, axis=-1, keepdims=True) + EPS_SUBLN) * sg_ref[...]
        o_ref[0] = (out * (1.0 - LAMBDA_INIT)).astype(o_ref.dtype)

    @pl.when(qb % 2 == 0)
    def _():
        step(0, 1)

    @pl.when(qb % 2 == 1)
    def _():
        step(1, 0)


def _diff_attn(qg3, lq1, lk1, lq2, lk2, subln_g, tq=256):
    bsz, seq, _ = qg3.shape
    w = DIFF_VDIM
    nq = seq // tq
    q0 = COL_QKV // w
    k0 = (COL_QKV + D_DIFF) // w
    v0 = (COL_QKV + 2 * D_DIFF) // w
    vec = lambda n: pl.BlockSpec((1, n), lambda b, h, i: (0, 0))
    return pl.pallas_call(
        functools.partial(_attn_body, tq=tq),
        grid=(bsz, N_DIFF_HEADS, nq),
        in_specs=[pl.BlockSpec((1, tq, w), lambda b, h, i: (b, i, q0 + h)),
                  pl.BlockSpec((1, tq, w), lambda b, h, i: (b, jnp.minimum(i + 1, nq - 1), q0 + h)),
                  pl.BlockSpec((1, seq, w), lambda b, h, i: (b, 0, k0 + h)),
                  pl.BlockSpec((1, seq, w), lambda b, h, i: (b, 0, v0 + h)),
                  vec(DIFF_HEAD), vec(DIFF_HEAD), vec(DIFF_HEAD), vec(DIFF_HEAD), vec(w)],
        out_specs=pl.BlockSpec((1, tq, w), lambda b, h, i: (b, i, h)),
        out_shape=jax.ShapeDtypeStruct((bsz, seq, D_DIFF), BF16),
        scratch_shapes=[pltpu.VMEM((2 * seq - tq, tq), F32),
                        pltpu.VMEM((w + ATTN_SUM_ROWS, seq), BF16),
                        pltpu.VMEM((2, seq, 2 * tq), F32),
                        pltpu.VMEM((seq, 2 * tq), BF16)],
        compiler_params=_cparams(("parallel", "parallel", "arbitrary")),
        name="diff_attn",
    )(qg3, qg3, qg3, qg3, lq1.reshape(1, -1), lk1.reshape(1, -1), lq2.reshape(1, -1), lk2.reshape(1, -1),
      subln_g.reshape(1, -1))


def _pad_cols(t, width):
    return jnp.pad(t, ((0, 0), (0, width - t.shape[1])))


def _pad_rows(t, rows):
    return jnp.pad(t, ((0, rows - t.shape[0]), (0, 0)))


def kernel(x, attn_pre_norm, attn_post_norm, w_in, shift_prev, shift_next, decay_bias_fwd, decay_up_fwd, decay_bias_bwd, decay_up_bwd, iclr_bias_fwd, iclr_up_fwd, iclr_bias_bwd, iclr_up_bwd, gate_up, k_k, k_a, r_k, ln_x_gain, ln_x_bias, lambda_q1, lambda_k1, lambda_q2, lambda_k2, subln_gain, w_up_rwkv, w_up_diff, w_out, mlp_pre_norm, mlp_post_norm, w_mlp_in, w_mlp_out):
    bsz, seq, d = x.shape
    m = bsz * seq
    l = 0
    x2 = x.reshape(m, d)

    wt = jnp.transpose(w_in[l])
    c_dw = 3 * D_RWKV
    c_da = c_dw + RANK_LORA
    c_dg = c_da + RANK_LORA
    c_q = c_dg + RANK_GATE
    wt_lora = jnp.concatenate([_pad_rows(wt[c_dw:c_da], RANK_LORA_PAD), _pad_rows(wt[c_da:c_dg], RANK_LORA_PAD),
                               wt[c_dg:c_q]], axis=0)

    def regroup(vec):
        t = vec.reshape(3, N_HEAD_GROUPS, HEAD_GROUP)
        return jnp.transpose(t, (1, 0, 2)).reshape(N_HEAD_GROUPS, 1, 3 * HEAD_GROUP)

    def lora_vec(vec):
        t = vec.reshape(1, -1)
        return jnp.concatenate([_pad_cols(t[:, c_dw:c_da], RANK_LORA_PAD), _pad_cols(t[:, c_da:c_dg], RANK_LORA_PAD),
                                t[:, c_dg:c_q]], axis=1)

    row = lambda t: t.reshape(1, -1)
    vecs = dict(
        mu_prev3=regroup(shift_prev[l][:c_dw]), mu_next3=regroup(shift_next[l][:c_dw]),
        lmu_prev=lora_vec(shift_prev[l]), lmu_next=lora_vec(shift_next[l]),
        w0f=row(decay_bias_fwd[l]), w0b=row(decay_bias_bwd[l]), a0f=row(iclr_bias_fwd[l]), a0b=row(iclr_bias_bwd[l]),
        k_k=row(k_k[l]), k_a=row(k_a[l]), r_k=row(r_k[l]))
    mats = dict(
        w2f=_pad_rows(decay_up_fwd[l], RANK_LORA_PAD).astype(BF16), w2b=_pad_rows(decay_up_bwd[l], RANK_LORA_PAD).astype(BF16),
        a2f=_pad_rows(iclr_up_fwd[l], RANK_LORA_PAD).astype(BF16), a2b=_pad_rows(iclr_up_bwd[l], RANK_LORA_PAD).astype(BF16),
        g2=gate_up[l].astype(BF16))

    h = _prenorm(x2, attn_pre_norm[l])
    col_scale = jnp.ones((1, N_PROJ), F32).at[:, COL_QKV:COL_QKV + D_DIFF].set(DIFF_HEAD ** -0.5 * LOG2E)
    qg = _matmul(h, wt, BF16, tm=2048, tn=512, col_scale=col_scale, bt_rows=((0, c_dw), (c_q, N_PROJ - c_dw)),
                 name="in_proj")
    rkv3 = qg.reshape(bsz, seq, N_PROJ)
    lora3 = _matmul(h, wt_lora, BF16, tm=1024, tn=LORA_W, bt_rows=((0, LORA_W),),
                    name="in_proj_lora").reshape(bsz, seq, LORA_W)

    r, v, kk, kd_f, kd_b, beta_f, beta_b, lw_f, lw_b, g, bonus = _rwkv_prep(rkv3, lora3, vecs, mats)
    o_f, o_b = _rwkv_scan(r, v, kk, kd_f, kd_b, beta_f, beta_b, lw_f, lw_b)
    y_a = _rwkv_post(o_f, o_b, bonus, g, row(ln_x_gain[l]), row(ln_x_bias[l]))

    y_b = _diff_attn(rkv3, lambda_q1[l], lambda_k1[l], lambda_q2[l], lambda_k2[l],
                     subln_gain[l])

    mixed = _upgate(y_a.reshape(m, D_RWKV), y_b.reshape(m, D_DIFF),
                    w_up_rwkv[l], w_up_diff[l], qg)
    z = _matmul(mixed, w_out[l], BF16, tm=2048, tn=512, name="out_proj")
    x1, h2 = _resnorm(x2, z, attn_post_norm[l], mlp_pre_norm[l])

    u = _matmul(h2, w_mlp_in[l], BF16, tm=2048, tn=512, act="relu2", name="mlp_in")
    z2 = _matmul_ktiled(u, w_mlp_out[l], BF16, tm=1024, tn=1024, tk=2048, name="mlp_out")
    out = _resnorm(x1, z2, mlp_post_norm[l])
    return out.reshape(bsz, seq, d)
```
